```python
import math
import jax, jax.numpy as jnp
from jax import lax
import numpy as np

D_MODEL = 2048
BATCH = 1
SEQ = 16384
DEPTH = 2
DEC_BATCH = 8
DEC_SEQ = 64
PAST_LEN = 1024

CHUNK = 64
N_A = DEPTH // 2
N_B = DEPTH - N_A
MEM_LEN = 256
HEAD_DIM = 128
MEM_HEADS = 4
MEM_WIDTH = MEM_HEADS * HEAD_DIM
MIX_WIDTH = D_MODEL - MEM_WIDTH
GDN_HEADS = MIX_WIDTH // HEAD_DIM
CONV_WIDTH = 4
CONV_DIM = 3 * MIX_WIDTH
DIFF_HEADS = MIX_WIDTH // (2 * HEAD_DIM)
DIFF_DK = HEAD_DIM
DIFF_DV = 2 * HEAD_DIM
D_FF = 5632
N_BUCKETS = 32
MAX_DISTANCE = 128
Q_BLOCK = 128
EPS = 1e-6
IN_A = CONV_DIM + MIX_WIDTH + 2 * GDN_HEADS + MEM_WIDTH
IN_B = 2 * DIFF_HEADS * DIFF_DK + MEM_WIDTH
KV_B = 2 * DIFF_HEADS * DIFF_DK + DIFF_HEADS * DIFF_DV

kernel_name = 'yoco_gdn_diffattn_stream_step'


def rmsnorm(x, g):
    xf = x.astype(jnp.float32)
    y = xf * lax.rsqrt(jnp.mean(xf * xf, axis=-1, keepdims=True) + EPS)
    return (y * g.astype(jnp.float32)).astype(x.dtype)


def l2norm(x):
    return x * lax.rsqrt(jnp.sum(x * x, axis=-1, keepdims=True) + EPS)


def swiglu(x, w_gu, w_down):
    gate, up = jnp.split(x @ w_gu, 2, axis=-1)
    return (jax.nn.silu(gate) * up) @ w_down


def causal_conv(x, buf, w):
    T = x.shape[1]
    xp = jnp.concatenate([buf.astype(x.dtype), x], axis=1)
    y = sum(xp[:, j:j + T] * w[j] for j in range(CONV_WIDTH))
    return jax.nn.silu(y), xp[:, T:]


def gated_delta_rule(q, k, v, g, beta, state):
    B, T, H, DK = k.shape
    pad = (-T) % CHUNK

    def to_chunks(a):
        a = jnp.pad(a, [(0, 0), (0, pad)] + [(0, 0)] * (a.ndim - 2))
        a = a.reshape((B, (T + pad) // CHUNK, CHUNK) + a.shape[2:])
        return jnp.moveaxis(a, 3, 1)

    q, k, v, g, beta = (to_chunks(a) for a in (q, k, v, g, beta))
    q = q * DK ** -0.5
    g = jnp.cumsum(g, axis=-1)
    incl = jnp.tril(jnp.ones((CHUNK, CHUNK), dtype=bool))
    strict = jnp.tril(jnp.ones((CHUNK, CHUNK), dtype=bool), -1)
    decay = jnp.where(incl, jnp.exp(jnp.where(incl, g[..., :, None] - g[..., None, :], 0.0)), 0.0)
    k_beta = k * beta[..., None]
    L = jnp.where(strict, jnp.einsum('bhncd,bhnsd->bhncs', k_beta, k) * decay, 0.0)
    eye = jnp.eye(CHUNK, dtype=L.dtype)
    t_inv = lax.linalg.triangular_solve(eye + L, jnp.broadcast_to(eye, L.shape),
                                        left_side=True, lower=True, unit_diagonal=True)
    u = jnp.einsum('bhncs,bhnse->bhnce', t_inv, v * beta[..., None])
    w = jnp.einsum('bhncs,bhnsd->bhncd', t_inv, k_beta * jnp.exp(g)[..., None])
    a_intra = jnp.where(incl, jnp.einsum('bhncd,bhnsd->bhncs', q, k) * decay, 0.0)
    g_last = g[..., -1:]
    xs = (q * jnp.exp(g)[..., None], k * jnp.exp(g_last - g)[..., None], u, w, a_intra,
          jnp.exp(g_last[..., 0]))
    xs = tuple(jnp.moveaxis(a, 2, 0) for a in xs)

    def step(S, inp):
        qg, kd, u_c, w_c, a_c, gl = inp
        v_new = u_c - jnp.einsum('bhcd,bhde->bhce', w_c, S)
        o = jnp.einsum('bhcd,bhde->bhce', qg, S) + jnp.einsum('bhcs,bhse->bhce', a_c, v_new)
        S = S * gl[..., None, None] + jnp.einsum('bhcd,bhce->bhde', kd, v_new)
        return S, o

    S, o = lax.scan(step, state, xs)
    o = jnp.transpose(o, (1, 0, 3, 2, 4)).reshape(B, T + pad, H, -1)[:, :T]
    return o, S


def memory_kv(mem, g, w):
    B, M, _ = mem.shape
    k, v = jnp.split(rmsnorm(mem, g) @ w, 2, axis=-1)
    return k.reshape(B, M, MEM_HEADS, HEAD_DIM), v.reshape(B, M, MEM_HEADS, HEAD_DIM)


def memory_attend(qm, mem_k, mem_v):
    B, T, _ = qm.shape
    q = qm.reshape(B, T, MEM_HEADS, HEAD_DIM)
    s = jnp.einsum('bthd,bmhd->bhtm', q, mem_k.astype(q.dtype)).astype(jnp.float32) * HEAD_DIM ** -0.5
    p = jax.nn.softmax(s, axis=-1).astype(q.dtype)
    return jnp.einsum('bhtm,bmhd->bthd', p, mem_v.astype(q.dtype)).reshape(B, T, MEM_WIDTH)


def rel_bucket(rel):
    nb = N_BUCKETS // 2
    max_exact = nb // 2
    n = jnp.abs(rel)
    large = max_exact + (jnp.log(jnp.maximum(n, 1).astype(jnp.float32) / max_exact)
                         / math.log(MAX_DISTANCE / max_exact) * (nb - max_exact)).astype(jnp.int32)
    large = jnp.minimum(large, nb - 1)
    return jnp.where(rel > 0, nb, 0) + jnp.where(n < max_exact, n, large)


def diff_attend(q, k, v, q_pos, k_pos, lam, rel_table):
    s = jnp.einsum('bqhmd,bkhmd->bhmqk', q, k).astype(jnp.float32) * DIFF_DK ** -0.5
    rel = k_pos[None, :] - q_pos[:, None]
    bias = jnp.transpose(rel_table[rel_bucket(rel)], (2, 0, 1)).astype(jnp.float32)
    visible = (k_pos[None, :] // CHUNK) <= (q_pos[:, None] // CHUNK)
    s = jnp.where(visible, s + bias[None, :, None], -1e30)
    p = jax.nn.softmax(s, axis=-1)
    a = p[:, :, 0] - lam * p[:, :, 1]
    return jnp.einsum('bhqk,bkhe->bqhe', a.astype(v.dtype), v)


def lambda_init(layer_idx):
    return 0.8 - 0.6 * math.exp(-0.3 * layer_idx)


def mixer_a(xn, conv_buf, S0, w_in, conv_w, a_log, dt_bias, onorm, mem_k, mem_v):
    B, T, _ = xn.shape
    proj = xn @ w_in
    qkv, z, b_logit, a_logit, qm = jnp.split(
        proj, [CONV_DIM, CONV_DIM + MIX_WIDTH, CONV_DIM + MIX_WIDTH + GDN_HEADS,
               CONV_DIM + MIX_WIDTH + 2 * GDN_HEADS], axis=-1)
    qkv, new_buf = causal_conv(qkv, conv_buf, conv_w)
    q, k, v = (t.reshape(B, T, GDN_HEADS, HEAD_DIM).astype(jnp.float32) for t in jnp.split(qkv, 3, axis=-1))
    q, k = l2norm(q), l2norm(k)
    beta = jax.nn.sigmoid(b_logit.astype(jnp.float32))
    g = -jnp.exp(a_log.astype(jnp.float32)) * jax.nn.softplus(a_logit.astype(jnp.float32) + dt_bias.astype(jnp.float32))
    o, S = gated_delta_rule(q, k, v, g, beta, S0.astype(jnp.float32))
    o = rmsnorm(o, onorm) * jax.nn.silu(z.reshape(B, T, GDN_HEADS, HEAD_DIM).astype(jnp.float32))
    o = o.reshape(B, T, MIX_WIDTH).astype(xn.dtype)
    m = memory_attend(qm, mem_k, mem_v)
    return jnp.concatenate([o, m], axis=-1), new_buf, S.astype(xn.dtype)


def mixer_b(xn, q_pos, k_all, v_all, k_pos, w_in, lam_qk, subln, lam0, rel_table, mem_k, mem_v, blocked):
    B, T, _ = xn.shape
    proj = xn @ w_in
    q = proj[..., :2 * DIFF_HEADS * DIFF_DK].reshape(B, T, DIFF_HEADS, 2, DIFF_DK)
    qm = proj[..., 2 * DIFF_HEADS * DIFF_DK:]
    lq = lam_qk.astype(jnp.float32)
    lam = jnp.exp(jnp.sum(lq[0] * lq[1])) - jnp.exp(jnp.sum(lq[2] * lq[3])) + lam0
    if blocked:
        nb = T // Q_BLOCK
        qb = jnp.moveaxis(q.reshape(B, nb, Q_BLOCK, DIFF_HEADS, 2, DIFF_DK), 1, 0)
        pb = q_pos.reshape(nb, Q_BLOCK)
        o = lax.map(lambda a: diff_attend(a[0], k_all, v_all, a[1], k_pos, lam, rel_table), (qb, pb))
        o = jnp.moveaxis(o, 0, 1).reshape(B, T, DIFF_HEADS, DIFF_DV)
    else:
        o = diff_attend(q, k_all, v_all, q_pos, k_pos, lam, rel_table)
    o = (rmsnorm(o, subln) * (1.0 - lam0)).reshape(B, T, MIX_WIDTH)
    m = memory_attend(qm, mem_k, mem_v)
    return jnp.concatenate([o, m], axis=-1)


def run_group(x, q_pos, mem_k, mem_v, conv_states, delta_states, past_k, past_v, blocked, W):
    B, T, _ = x.shape
    h = x
    new_conv, new_delta = [], []
    for l in range(DEPTH):
        ng = W['norm_gains'][l]
        if l == N_A:
            kv = rmsnorm(h, W['kv_norm']) @ W['w_kv']
            k_new = kv[..., :2 * DIFF_HEADS * DIFF_DK].reshape(B, T, DIFF_HEADS, 2, DIFF_DK)
            v_new = kv[..., 2 * DIFF_HEADS * DIFF_DK:].reshape(B, T, DIFF_HEADS, DIFF_DV)
            if past_k is None:
                k_all, v_all = k_new, v_new
            else:
                k_all = jnp.concatenate([past_k.astype(k_new.dtype), k_new], axis=1)
                v_all = jnp.concatenate([past_v.astype(v_new.dtype), v_new], axis=1)
            k_pos = jnp.arange(k_all.shape[1])
        h = h + 0.5 * rmsnorm(swiglu(rmsnorm(h, ng[0]), W['ffn_gate_up'][l, 0], W['ffn_down'][l, 0]), ng[1])
        xn = rmsnorm(h, ng[2])
        if l < N_A:
            mix, cb, S = mixer_a(xn, conv_states[l], delta_states[l], W['w_in_a'][l], W['conv_w_a'][l],
                                 W['a_log'][l], W['dt_bias'][l], W['onorm_a'][l], mem_k[l], mem_v[l])
            new_conv.append(cb)
            new_delta.append(S)
        else:
            j = l - N_A
            mix = mixer_b(xn, q_pos, k_all, v_all, k_pos, W['w_in_b'][j], W['lambda_qk'][j], W['subln_b'][j],
                          lambda_init(l), W['rel_bias'], mem_k[l], mem_v[l], blocked)
        h = h + rmsnorm(mix @ W['w_out'][l], ng[3])
        h = h + 0.5 * rmsnorm(swiglu(rmsnorm(h, ng[4]), W['ffn_gate_up'][l, 1], W['ffn_down'][l, 1]), ng[5])
    return h, jnp.stack(new_conv), jnp.stack(new_delta), k_new, v_new


def setup_inputs(seed: int = 0) -> dict:
    key = jax.random.key(seed)
    ks = jax.random.split(key, 32)

    def nrm(k, shape, scale=1.0):
        return jax.random.normal(k, shape, jnp.float32) * scale

    dt = jnp.exp(jax.random.uniform(ks[18], (N_A, GDN_HEADS), jnp.float32, math.log(1e-3), math.log(0.1)))
    return {
        'x_prompt': nrm(ks[0], (BATCH, SEQ, D_MODEL)),
        'x_sample': nrm(ks[1], (DEC_BATCH, DEC_SEQ, D_MODEL)),
        'mem_prompt': nrm(ks[2], (BATCH, MEM_LEN, D_MODEL)),
        'cache_k': nrm(ks[3], (DEC_BATCH, PAST_LEN, DIFF_HEADS, 2, DIFF_DK)),
        'cache_v': nrm(ks[4], (DEC_BATCH, PAST_LEN, DIFF_HEADS, DIFF_DV)),
        'cache_mem_k': nrm(ks[5], (DEPTH, DEC_BATCH, MEM_LEN, MEM_HEADS, HEAD_DIM)),
        'cache_mem_v': nrm(ks[6], (DEPTH, DEC_BATCH, MEM_LEN, MEM_HEADS, HEAD_DIM)),
        'state_delta': nrm(ks[7], (N_A, DEC_BATCH, GDN_HEADS, HEAD_DIM, HEAD_DIM), 0.1),
        'state_conv': nrm(ks[8], (N_A, DEC_BATCH, CONV_WIDTH - 1, CONV_DIM)),
        'norm_gains': 1.0 + nrm(ks[9], (DEPTH, 6, D_MODEL), 0.02),
        'ffn_gate_up': nrm(ks[10], (DEPTH, 2, D_MODEL, 2 * D_FF), D_MODEL ** -0.5),
        'ffn_down': nrm(ks[11], (DEPTH, 2, D_FF, D_MODEL), D_FF ** -0.5),
        'w_out': nrm(ks[12], (DEPTH, D_MODEL, D_MODEL), D_MODEL ** -0.5),
        'mem_norm': 1.0 + nrm(ks[13], (DEPTH, D_MODEL), 0.02),
        'w_mem_kv': nrm(ks[14], (DEPTH, D_MODEL, 2 * MEM_WIDTH), D_MODEL ** -0.5),
        'w_in_a': nrm(ks[15], (N_A, D_MODEL, IN_A), D_MODEL ** -0.5),
        'conv_w_a': nrm(ks[16], (N_A, CONV_WIDTH, CONV_DIM), CONV_WIDTH ** -0.5),
        'a_log': jnp.log(jax.random.uniform(ks[17], (N_A, GDN_HEADS), jnp.float32, 1.0, 16.0)),
        'dt_bias': dt + jnp.log(-jnp.expm1(-dt)),
        'onorm_a': 1.0 + nrm(ks[19], (N_A, HEAD_DIM), 0.02),
        'kv_norm': 1.0 + nrm(ks[20], (D_MODEL,), 0.02),
        'w_kv': nrm(ks[21], (D_MODEL, KV_B), D_MODEL ** -0.5),
        'w_in_b': nrm(ks[22], (N_B, D_MODEL, IN_B), D_MODEL ** -0.5),
        'lambda_qk': nrm(ks[23], (N_B, 4, DIFF_DK), 0.1),
        'subln_b': 1.0 + nrm(ks[24], (N_B, DIFF_DV), 0.02),
        'rel_bias': nrm(ks[25], (N_BUCKETS, DIFF_HEADS), 0.5),
    }


def reference(x_prompt, x_sample, mem_prompt, cache_k, cache_v, cache_mem_k, cache_mem_v, state_delta,
              state_conv, norm_gains, ffn_gate_up, ffn_down, w_out, mem_norm, w_mem_kv, w_in_a, conv_w_a,
              a_log, dt_bias, onorm_a, kv_norm, w_kv, w_in_b, lambda_qk, subln_b, rel_bias):
    W = {'norm_gains': norm_gains, 'ffn_gate_up': ffn_gate_up, 'ffn_down': ffn_down, 'w_out': w_out,
         'w_in_a': w_in_a, 'conv_w_a': conv_w_a, 'a_log': a_log, 'dt_bias': dt_bias, 'onorm_a': onorm_a,
         'kv_norm': kv_norm, 'w_kv': w_kv, 'w_in_b': w_in_b, 'lambda_qk': lambda_qk, 'subln_b': subln_b,
         'rel_bias': rel_bias}
    mem_kv_p = [memory_kv(mem_prompt, mem_norm[l], w_mem_kv[l]) for l in range(DEPTH)]
    mem_k_p = jnp.stack([kv[0] for kv in mem_kv_p])
    mem_v_p = jnp.stack([kv[1] for kv in mem_kv_p])
    bp = x_prompt.shape[0]
    conv0 = jnp.zeros((N_A, bp, CONV_WIDTH - 1, CONV_DIM), x_prompt.dtype)
    delta0 = jnp.zeros((N_A, bp, GDN_HEADS, HEAD_DIM, HEAD_DIM), x_prompt.dtype)
    y_prompt, conv_p, delta_p, k_p, v_p = run_group(
        x_prompt, jnp.arange(x_prompt.shape[1]), mem_k_p, mem_v_p, conv0, delta0, None, None, True, W)
    q_pos_s = cache_k.shape[1] + jnp.arange(x_sample.shape[1])
    y_sample, conv_s, delta_s, k_s, v_s = run_group(
        x_sample, q_pos_s, cache_mem_k, cache_mem_v, state_conv, state_delta, cache_k, cache_v, False, W)
    return (y_prompt, y_sample, delta_p, conv_p, k_p, v_p, mem_k_p, mem_v_p, delta_s, conv_s, k_s, v_s)
```

```python
import functools
import math

import jax
import jax.numpy as jnp
from jax import lax
from jax.experimental import pallas as pl
from jax.experimental.pallas import tpu as pltpu

F32 = jnp.float32
BF16 = jnp.bfloat16

EPS = 1e-6
CHUNK = 64
HEAD_DIM = 128
MEM_HEADS = 4
MEM_WIDTH = MEM_HEADS * HEAD_DIM
CONV_WIDTH = 4
N_BUCKETS = 32
MAX_DISTANCE = 128
NEG_INF = -1e30

LANES = 128
SUBLANES = 8
VMEM_LIMIT = 56 * 1024 * 1024

ROW_TILE = 512
ATTN_TILE = 512
GDN_CHUNKS = 4


def _params(*sem):
    return pltpu.CompilerParams(dimension_semantics=sem, vmem_limit_bytes=VMEM_LIMIT)


def _dot(a, b, precision=None):
    return jnp.dot(a, b, preferred_element_type=F32, precision=precision)


def _dot_nt(a, b):
    return lax.dot_general(a, b, (((1,), (1,)), ((), ())), preferred_element_type=F32)


def _dot_tn(a, b):
    return lax.dot_general(a, b, (((0,), (0,)), ((), ())), preferred_element_type=F32)


def _rms(x, gain):
    return x * lax.rsqrt(jnp.mean(x * x, axis=-1, keepdims=True) + EPS) * gain


def _silu(x):
    return x * jax.nn.sigmoid(x)


def _row_tile(rows, want):
    t = min(want, rows)
    assert rows % t == 0, (rows, t)
    return t


def _ffn_body(x_ref, g0_ref, g1_ref, wg_ref, wu_ref, wd_ref, o_ref, xn_ref, acc_ref):
    f = pl.program_id(1)

    @pl.when(f == 0)
    def _():
        xn_ref[...] = _rms(x_ref[...], g0_ref[...]).astype(BF16)

    xn = xn_ref[...]
    gate = _dot(xn, wg_ref[...])
    up = _dot(xn, wu_ref[...])
    part = _dot((_silu(gate) * up).astype(BF16), wd_ref[...])

    @pl.when(f == 0)
    def _():
        acc_ref[...] = part

    @pl.when(f != 0)
    def _():
        acc_ref[...] += part

    @pl.when(f == pl.num_programs(1) - 1)
    def _():
        o_ref[...] = x_ref[...] + 0.5 * _rms(acc_ref[...], g1_ref[...])


def _ffn(x, g0, g1, w_gu, w_down, *, tf=512):
    rows, d = x.shape
    dff = w_down.shape[0]
    tm = _row_tile(rows, ROW_TILE)
    tf = _row_tile(dff, tf)
    nf = dff // tf
    return pl.pallas_call(
        _ffn_body,
        grid=(rows // tm, nf),
        in_specs=[
            pl.BlockSpec((tm, d), lambda i, f: (i, 0)),
            pl.BlockSpec((1, d), lambda i, f: (0, 0)),
            pl.BlockSpec((1, d), lambda i, f: (0, 0)),
            pl.BlockSpec((d, tf), lambda i, f: (0, f)),
            pl.BlockSpec((d, tf), lambda i, f: (0, f + nf)),
            pl.BlockSpec((tf, d), lambda i, f: (f, 0)),
        ],
        out_specs=pl.BlockSpec((tm, d), lambda i, f: (i, 0)),
        out_shape=jax.ShapeDtypeStruct((rows, d), F32),
        scratch_shapes=[pltpu.VMEM((tm, d), BF16), pltpu.VMEM((tm, d), F32)],
        compiler_params=_params("parallel", "arbitrary"),
        name="ffn",
    )(x, g0.reshape(1, d), g1.reshape(1, d), w_gu, w_gu, w_down)


def _proj_body(n_main, has_side, *refs):
    x_ref, g_ref = refs[0], refs[1]
    w_refs = refs[2:2 + n_main]
    pos = 2 + n_main
    ws_ref = refs[pos] if has_side else None
    pos += int(has_side)
    o_refs = refs[pos:pos + n_main]
    pos += n_main
    os_ref = refs[pos] if has_side else None
    pos += int(has_side)
    xn_ref = refs[pos]

    @pl.when(pl.program_id(1) == 0)
    def _():
        xn_ref[...] = _rms(x_ref[...], g_ref[...]).astype(BF16)
        if has_side:
            os_ref[...] = _dot(xn_ref[...], ws_ref[...])

    xn = xn_ref[...]
    for w_ref, o_ref in zip(w_refs, o_refs):
        o_ref[...] = _dot(xn, w_ref[...])


def _proj(x, g, w_mains, w_side=None, *, tn=512, name="proj"):
    rows, d = x.shape
    n = w_mains[0].shape[1]
    assert all(w.shape == (d, n) for w in w_mains)
    tm = _row_tile(rows, ROW_TILE)
    tn = _row_tile(n, tn)
    has_side = w_side is not None
    in_specs = [pl.BlockSpec((tm, d), lambda i, j: (i, 0)), pl.BlockSpec((1, d), lambda i, j: (0, 0))]
    in_specs += [pl.BlockSpec((d, tn), lambda i, j: (0, j)) for _ in w_mains]
    out_specs = [pl.BlockSpec((tm, tn), lambda i, j: (i, j)) for _ in w_mains]
    out_shape = [jax.ShapeDtypeStruct((rows, n), F32) for _ in w_mains]
    args = [x, g.reshape(1, d), *w_mains]
    if has_side:
        ns = w_side.shape[1]
        in_specs.append(pl.BlockSpec((d, ns), lambda i, j: (0, 0)))
        out_specs.append(pl.BlockSpec((tm, ns), lambda i, j: (i, 0)))
        out_shape.append(jax.ShapeDtypeStruct((rows, ns), F32))
        args.append(w_side)
    return pl.pallas_call(
        functools.partial(_proj_body, len(w_mains), has_side),
        grid=(rows // tm, n // tn),
        in_specs=in_specs,
        out_specs=out_specs,
        out_shape=out_shape,
        scratch_shapes=[pltpu.VMEM((tm, d), BF16)],
        compiler_params=_params("parallel", "arbitrary"),
        name=name,
    )(*args)


def _gdn_body(nc, q_ref, k_ref, v_ref, z_ref, ba_ref, cq_ref, ck_ref, cv_ref, wq_ref, wk_ref, wv_ref,
              alog_ref, dtb_ref, onorm_ref, s0_ref, o_ref, sout_ref, xs_ref, state_ref):
    h = pl.program_id(1)
    t = pl.program_id(2)
    tr = nc * CHUNK
    halo = SUBLANES

    @pl.when(t == 0)
    def _():
        xs_ref[0, 0:halo, :] = cq_ref[...]
        xs_ref[1, 0:halo, :] = ck_ref[...]
        xs_ref[2, 0:halo, :] = cv_ref[...]
        state_ref[...] = s0_ref[...]

    def conv(c, x_ref, w_ref):
        xs_ref[c, halo:halo + tr, :] = x_ref[...]
        w = w_ref[...]
        y = sum(xs_ref[c, halo - 3 + j:halo - 3 + j + tr, :] * w[j:j + 1, :] for j in range(CONV_WIDTH))
        xs_ref[c, 0:halo, :] = xs_ref[c, tr:tr + halo, :]
        return _silu(y)

    q = conv(0, q_ref, wq_ref)
    k = conv(1, k_ref, wk_ref)
    v = conv(2, v_ref, wv_ref)
    q = q * lax.rsqrt(jnp.sum(q * q, axis=-1, keepdims=True) + EPS) * (HEAD_DIM ** -0.5)
    k = k * lax.rsqrt(jnp.sum(k * k, axis=-1, keepdims=True) + EPS)

    ba = ba_ref[...]
    lane = lax.broadcasted_iota(jnp.int32, ba.shape, 1)
    n_heads = pl.num_programs(1)
    b_col = jnp.sum(jnp.where(lane == h, ba, 0.0), axis=1, keepdims=True)
    g_all = -jnp.exp(alog_ref[...]) * jax.nn.softplus(ba + dtb_ref[...])
    g_col = jnp.sum(jnp.where(lane == h + n_heads, g_all, 0.0), axis=1, keepdims=True)
    beta = jax.nn.sigmoid(b_col)

    row = lax.broadcasted_iota(jnp.int32, (CHUNK, CHUNK), 0)
    col = lax.broadcasted_iota(jnp.int32, (CHUNK, CHUNK), 1)
    incl = row >= col
    strict = row > col
    eye = jnp.where(row == col, 1.0, 0.0).astype(F32)
    hi = lax.Precision.HIGHEST

    state = state_ref[...]
    for c in range(nc):
        sl = slice(c * CHUNK, (c + 1) * CHUNK)
        qc, kc, vc, bc, gc = q[sl], k[sl], v[sl], beta[sl], g_col[sl]
        g_row = jnp.sum(jnp.where(row == col, gc, 0.0), axis=0, keepdims=True)
        cum_col = jnp.sum(jnp.where(incl, g_row, 0.0), axis=1, keepdims=True)
        cum_row = jnp.sum(jnp.where(row <= col, gc, 0.0), axis=0, keepdims=True)
        g_last = jnp.sum(gc, axis=0, keepdims=True)
        decay = jnp.where(incl, jnp.exp(jnp.where(incl, cum_col - cum_row, 0.0)), 0.0)

        kb = kc * bc
        kb16 = kb.astype(BF16)
        k16 = kc.astype(BF16)
        lower = jnp.where(strict, _dot_nt(kb16, k16) * decay, 0.0)
        power = -lower
        t_inv = eye + power
        for _ in range(5):
            power = _dot(power, power, hi)
            t_inv = t_inv + _dot(t_inv, power, hi)

        e_cum = jnp.exp(cum_col)
        t16 = t_inv.astype(BF16)
        u = _dot(t16, (vc * bc).astype(BF16))
        w = _dot(t16, (kb * e_cum).astype(BF16))
        a_intra = jnp.where(incl, _dot_nt(qc.astype(BF16), k16) * decay, 0.0)
        qg = qc * e_cum
        kd = kc * jnp.exp(g_last - cum_col)

        s16 = state.astype(BF16)
        v_new = u - _dot(w.astype(BF16), s16)
        vn16 = v_new.astype(BF16)
        o = _dot(qg.astype(BF16), s16) + _dot(a_intra.astype(BF16), vn16)
        state = state * jnp.exp(g_last) + _dot_tn(kd.astype(BF16), vn16)

        o_ref[sl, :] = _rms(o, onorm_ref[...]) * _silu(z_ref[sl, :])

    state_ref[...] = state
    sout_ref[...] = state


def _gdn(proj, ba, conv_buf, conv_w, a_log, dt_bias, onorm, s0, *, nc):
    bsz, tlen, _ = proj.shape
    n_heads = s0.shape[1]
    tr = nc * CHUNK
    assert tlen % tr == 0
    nt = tlen // tr
    pad = lambda a: jnp.zeros((1, LANES), F32).at[0, n_heads:2 * n_heads].set(a)

    def col(c):
        return pl.BlockSpec((None, tr, HEAD_DIM), lambda b, h, t: (b, t, c * n_heads + h))

    def buf(c):
        return pl.BlockSpec((None, SUBLANES, HEAD_DIM), lambda b, h, t: (b, 0, c * n_heads + h))

    def tap(c):
        return pl.BlockSpec((CONV_WIDTH, HEAD_DIM), lambda b, h, t: (0, c * n_heads + h))

    vec = pl.BlockSpec((1, LANES), lambda b, h, t: (0, 0))
    state_spec = pl.BlockSpec((None, None, HEAD_DIM, HEAD_DIM), lambda b, h, t: (b, h, 0, 0))
    return pl.pallas_call(
        functools.partial(_gdn_body, nc),
        grid=(bsz, n_heads, nt),
        in_specs=[col(0), col(1), col(2), col(3),
                  pl.BlockSpec((None, tr, LANES), lambda b, h, t: (b, t, 0)),
                  buf(0), buf(1), buf(2), tap(0), tap(1), tap(2), vec, vec, vec, state_spec],
        out_specs=[pl.BlockSpec((None, tr, HEAD_DIM), lambda b, h, t: (b, t, h)), state_spec],
        out_shape=[jax.ShapeDtypeStruct((bsz, tlen, n_heads * HEAD_DIM), F32),
                   jax.ShapeDtypeStruct(s0.shape, F32)],
        scratch_shapes=[pltpu.VMEM((3, tr + SUBLANES, HEAD_DIM), F32), pltpu.VMEM((HEAD_DIM, HEAD_DIM), F32)],
        compiler_params=_params("parallel", "parallel", "arbitrary"),
        name="gdn",
    )(proj, proj, proj, proj, ba, conv_buf, conv_buf, conv_buf, conv_w, conv_w, conv_w,
      pad(a_log), pad(dt_bias), onorm.reshape(1, HEAD_DIM), s0)


def _diff_body(causal, lam0, q_ref, k_ref, v_ref, bias_ref, lq_ref, sub_ref, o_ref,
               qs_ref, m_ref, l_ref, acc_ref):
    i = pl.program_id(2)
    j = pl.program_id(3)
    last = i if causal else pl.num_programs(3) - 1
    tk = k_ref.shape[0]
    dk = HEAD_DIM

    @pl.when(j == 0)
    def _():
        q = q_ref[...] * (dk ** -0.5)
        qs_ref[0] = q[:, :dk].astype(BF16)
        qs_ref[1] = q[:, dk:].astype(BF16)
        m_ref[...] = jnp.full(m_ref.shape, NEG_INF, F32)
        l_ref[...] = jnp.zeros(l_ref.shape, F32)
        acc_ref[...] = jnp.zeros(acc_ref.shape, F32)

    @pl.when(j <= last)
    def _():
        bias = bias_ref[...]
        v = v_ref[...]
        for mp in range(2):
            s = _dot_nt(qs_ref[mp], k_ref[:, mp * dk:(mp + 1) * dk]) + bias
            m_prev = m_ref[mp]
            m_next = jnp.maximum(m_prev, jnp.max(s, axis=1, keepdims=True))
            p = jnp.exp(s - jnp.concatenate([m_next] * (tk // LANES), axis=1))
            alpha = jnp.exp(m_prev - m_next)
            l_ref[mp] = alpha * l_ref[mp] + jnp.sum(p, axis=1, keepdims=True)
            m_ref[mp] = m_next
            acc_ref[mp] = (acc_ref[mp] * jnp.concatenate([alpha] * (2 * dk // LANES), axis=1)
                           + _dot(p.astype(BF16), v))

    @pl.when(j == last)
    def _():
        lq = lq_ref[...]
        lam = (jnp.exp(jnp.sum(lq[0:1] * lq[1:2], axis=1, keepdims=True))
               - jnp.exp(jnp.sum(lq[2:3] * lq[3:4], axis=1, keepdims=True)) + lam0)
        reps = 2 * dk // LANES
        o0 = acc_ref[0] / jnp.concatenate([l_ref[0]] * reps, axis=1)
        o1 = acc_ref[1] / jnp.concatenate([l_ref[1]] * reps, axis=1)
        o_ref[...] = _rms(o0 - lam * o1, sub_ref[...]) * (1.0 - lam0)


def _diff_attention(q, k, v, bias, lam_qk, subln, lam0, *, n_heads, tq, tk, causal):
    bsz, tq_len, _ = q.shape
    tk_len = k.shape[1]
    nq, nk = tq_len // tq, tk_len // tk
    dv = 2 * HEAD_DIM
    if causal:
        assert tq == tk
        kv_map = lambda b, h, i, j: (b, jnp.minimum(j, i), h)
        bias_map = lambda b, h, i, j: (h, jnp.clip(j - i + 2, 0, 2), 0, 0)
    else:
        kv_map = lambda b, h, i, j: (b, j, h)
        bias_map = lambda b, h, i, j: (h, j, 0, 0)
    return pl.pallas_call(
        functools.partial(_diff_body, causal, lam0),
        grid=(bsz, n_heads, nq, nk),
        in_specs=[
            pl.BlockSpec((None, tq, dv), lambda b, h, i, j: (b, i, h)),
            pl.BlockSpec((None, tk, dv), kv_map),
            pl.BlockSpec((None, tk, dv), kv_map),
            pl.BlockSpec((None, None, tq, tk), bias_map),
            pl.BlockSpec((4, HEAD_DIM), lambda b, h, i, j: (0, 0)),
            pl.BlockSpec((1, dv), lambda b, h, i, j: (0, 0)),
        ],
        out_specs=pl.BlockSpec((None, tq, dv), lambda b, h, i, j: (b, i, h)),
        out_shape=jax.ShapeDtypeStruct((bsz, tq_len, n_heads * dv), F32),
        scratch_shapes=[pltpu.VMEM((2, tq, HEAD_DIM), BF16), pltpu.VMEM((2, tq, LANES), F32),
                        pltpu.VMEM((2, tq, LANES), F32), pltpu.VMEM((2, tq, dv), F32)],
        compiler_params=_params("parallel", "parallel", "parallel", "arbitrary"),
        name="diff_attn",
    )(q, k, v, bias, lam_qk, subln.reshape(1, dv))


def _rel_bucket(rel):
    nb = N_BUCKETS // 2
    max_exact = nb // 2
    n = jnp.abs(rel)
    large = max_exact + (jnp.log(jnp.maximum(n, 1).astype(F32) / max_exact)
                         / math.log(MAX_DISTANCE / max_exact) * (nb - max_exact)).astype(jnp.int32)
    large = jnp.minimum(large, nb - 1)
    return jnp.where(rel > 0, nb, 0) + jnp.where(n < max_exact, n, large)


def _bias_window(rel_table, q_pos, k_pos):
    bias = jnp.transpose(rel_table[_rel_bucket(k_pos[None, :] - q_pos[:, None])], (2, 0, 1))
    visible = (k_pos[None, :] // CHUNK) <= (q_pos[:, None] // CHUNK)
    return jnp.where(visible[None], bias, NEG_INF).astype(F32)


def _mix_out_body(o_ref, qm_ref, mk_ref, mv_ref, w_ref, h_ref, g_ref, out_ref, mix_ref):
    n_o = o_ref.shape[1]
    mix_ref[:, 0:n_o] = o_ref[...].astype(BF16)
    for hd in range(MEM_HEADS):
        sl = slice(hd * HEAD_DIM, (hd + 1) * HEAD_DIM)
        s = _dot_nt(qm_ref[:, sl].astype(BF16), mk_ref[:, sl]) * (HEAD_DIM ** -0.5)
        e = jnp.exp(s - jnp.max(s, axis=1, keepdims=True))
        p = e / jnp.sum(e, axis=1, keepdims=True)
        mix_ref[:, n_o + hd * HEAD_DIM:n_o + (hd + 1) * HEAD_DIM] = _dot(p.astype(BF16), mv_ref[:, sl]).astype(BF16)
    y = _dot(mix_ref[...], w_ref[...])
    out_ref[...] = h_ref[...] + _rms(y, g_ref[...])


def _mix_out(o, qm_src, qm_block, mem_k, mem_v, w_out, h, g):
    bsz, tlen, n_o = o.shape
    d = h.shape[2]
    mlen = mem_k.shape[1]
    tm = _row_tile(tlen, ROW_TILE)
    return pl.pallas_call(
        _mix_out_body,
        grid=(bsz, tlen // tm),
        in_specs=[
            pl.BlockSpec((None, tm, n_o), lambda b, i: (b, i, 0)),
            pl.BlockSpec((None, tm, MEM_WIDTH), lambda b, i: (b, i, qm_block)),
            pl.BlockSpec((None, mlen, MEM_WIDTH), lambda b, i: (b, 0, 0)),
            pl.BlockSpec((None, mlen, MEM_WIDTH), lambda b, i: (b, 0, 0)),
            pl.BlockSpec((d, d), lambda b, i: (0, 0)),
            pl.BlockSpec((None, tm, d), lambda b, i: (b, i, 0)),
            pl.BlockSpec((1, d), lambda b, i: (0, 0)),
        ],
        out_specs=pl.BlockSpec((None, tm, d), lambda b, i: (b, i, 0)),
        out_shape=jax.ShapeDtypeStruct(h.shape, F32),
        scratch_shapes=[pltpu.VMEM((tm, d), BF16)],
        compiler_params=_params("parallel", "parallel"),
        name="mix_out",
    )(o, qm_src, mem_k, mem_v, w_out, h, g.reshape(1, d))


def _lambda_init(layer_idx):
    return 0.8 - 0.6 * math.exp(-0.3 * layer_idx)


def _prep_weights(norm_gains, ffn_gate_up, ffn_down, w_out, w_mem_kv, w_in_a, conv_w_a, w_kv, w_in_b):
    d = w_out.shape[1]
    n_a = w_in_a.shape[0]
    conv_dim = conv_w_a.shape[2]
    mix = d - MEM_WIDTH
    n_heads = mix // HEAD_DIM
    qkvz = conv_dim + mix
    w = {
        "gu": ffn_gate_up.astype(BF16), "down": ffn_down.astype(BF16), "out": w_out.astype(BF16),
        "mem_k": w_mem_kv[:, :, :MEM_WIDTH].astype(BF16), "mem_v": w_mem_kv[:, :, MEM_WIDTH:].astype(BF16),
        "in_a": jnp.concatenate([w_in_a[:, :, :qkvz], w_in_a[:, :, qkvz + 2 * n_heads:]], axis=2).astype(BF16),
        "in_a_ba": jnp.pad(w_in_a[:, :, qkvz:qkvz + 2 * n_heads],
                           ((0, 0), (0, 0), (0, LANES - 2 * n_heads))).astype(BF16),
        "kv_k": w_kv[:, :mix].astype(BF16), "kv_v": w_kv[:, mix:].astype(BF16),
        "in_b": w_in_b.astype(BF16),
    }
    return w, n_a, n_heads, mix


def _run_group(x, mem_k, mem_v, conv_states, delta_states, past_k, past_v, q_pos0, W, P, *, blocked):
    w, n_a, n_heads, mix = W
    bsz, tlen, d = x.shape
    rows = bsz * tlen
    depth = P["norm_gains"].shape[0]
    diff_heads = mix // (2 * HEAD_DIM)
    h = x.reshape(rows, d)
    new_conv, new_delta = [], []
    k_new = v_new = None
    for l in range(depth):
        ng = P["norm_gains"][l]
        if l == n_a:
            k_new, v_new = _proj(h, P["kv_norm"], [w["kv_k"], w["kv_v"]], name="proj_kv")
        h = _ffn(h, ng[0], ng[1], w["gu"][l, 0], w["down"][l, 0])
        if l < n_a:
            proj, ba = _proj(h, ng[2], [w["in_a"][l]], w["in_a_ba"][l], name="proj_a")
            n_proj = proj.shape[1]
            proj = proj.reshape(bsz, tlen, n_proj)
            conv_dim = 3 * mix
            new_conv.append(proj[:, tlen - (CONV_WIDTH - 1):, :conv_dim])
            conv_buf = jnp.pad(conv_states[l], ((0, 0), (SUBLANES - (CONV_WIDTH - 1), 0), (0, 0)))
            o, s_out = _gdn(proj, ba.reshape(bsz, tlen, LANES), conv_buf, P["conv_w_a"][l], P["a_log"][l],
                            P["dt_bias"][l], P["onorm_a"][l], delta_states[l], nc=GDN_CHUNKS if tlen % (GDN_CHUNKS * CHUNK) == 0 else 1)
            new_delta.append(s_out)
            qm_src, qm_block = proj, (conv_dim + mix) // MEM_WIDTH
        else:
            jb = l - n_a
            (proj,) = _proj(h, ng[2], [w["in_b"][jb]], name="proj_b")
            proj = proj.reshape(bsz, tlen, d)
            lam0 = _lambda_init(l)
            k3 = k_new.reshape(bsz, tlen, mix).astype(BF16)
            v3 = v_new.reshape(bsz, tlen, mix).astype(BF16)
            if blocked:
                t = _row_tile(tlen, ATTN_TILE)
                pos = jnp.arange(2 * t)
                near = _bias_window(P["rel_bias"], pos[t:], pos)
                far = jnp.broadcast_to(P["rel_bias"][_rel_bucket(jnp.int32(-2 * MAX_DISTANCE))][:, None, None],
                                       (diff_heads, t, t))
                bias = jnp.stack([far, near[:, :, :t], near[:, :, t:]], axis=1)
                o = _diff_attention(proj, k3, v3, bias, P["lambda_qk"][jb], P["subln_b"][jb], lam0,
                                    n_heads=diff_heads, tq=t, tk=t, causal=True)
            else:
                past = past_k.shape[1]
                total = past + tlen
                padded = -(-total // LANES) * LANES
                grow = lambda c, n: jnp.pad(jnp.concatenate([c.reshape(bsz, past, mix).astype(BF16), n], axis=1),
                                            ((0, 0), (0, padded - total), (0, 0)))
                k_pos = jnp.arange(padded)
                bias = _bias_window(P["rel_bias"], q_pos0 + jnp.arange(tlen), k_pos)
                bias = jnp.where(k_pos[None, None, :] < total, bias, NEG_INF)[:, None]
                o = _diff_attention(proj, grow(past_k, k3), grow(past_v, v3), bias, P["lambda_qk"][jb],
                                    P["subln_b"][jb], lam0, n_heads=diff_heads, tq=tlen, tk=padded, causal=False)
            qm_src, qm_block = proj, mix // MEM_WIDTH
        h = _mix_out(o, qm_src, qm_block, mem_k[l], mem_v[l], w["out"][l], h.reshape(bsz, tlen, d), ng[3])
        h = _ffn(h.reshape(rows, d), ng[4], ng[5], w["gu"][l, 1], w["down"][l, 1])
    return h.reshape(bsz, tlen, d), jnp.stack(new_conv), jnp.stack(new_delta), k_new, v_new


def kernel(x_prompt, x_sample, mem_prompt, cache_k, cache_v, cache_mem_k, cache_mem_v, state_delta, state_conv,
           norm_gains, ffn_gate_up, ffn_down, w_out, mem_norm, w_mem_kv, w_in_a, conv_w_a, a_log, dt_bias,
           onorm_a, kv_norm, w_kv, w_in_b, lambda_qk, subln_b, rel_bias):
    W = _prep_weights(norm_gains, ffn_gate_up, ffn_down, w_out, w_mem_kv, w_in_a, conv_w_a, w_kv, w_in_b)
    w, n_a, n_heads, mix = W
    P = {"norm_gains": norm_gains, "kv_norm": kv_norm, "conv_w_a": conv_w_a, "a_log": a_log, "dt_bias": dt_bias,
         "onorm_a": onorm_a, "lambda_qk": lambda_qk, "subln_b": subln_b, "rel_bias": rel_bias}
    depth = norm_gains.shape[0]
    bp, tp, d = x_prompt.shape
    bs, ts, _ = x_sample.shape
    mlen = mem_prompt.shape[1]
    diff_heads = mix // (2 * HEAD_DIM)

    mem_rows = mem_prompt.reshape(bp * mlen, d)
    mem_kv = [_proj(mem_rows, mem_norm[l], [w["mem_k"][l], w["mem_v"][l]], name="proj_mem") for l in range(depth)]
    mem_k_p = jnp.stack([kv[0] for kv in mem_kv]).reshape(depth, bp, mlen, MEM_HEADS, HEAD_DIM)
    mem_v_p = jnp.stack([kv[1] for kv in mem_kv]).reshape(depth, bp, mlen, MEM_HEADS, HEAD_DIM)
    conv0 = jnp.zeros((n_a, bp, CONV_WIDTH - 1, 3 * mix), F32)
    delta0 = jnp.zeros((n_a, bp, n_heads, HEAD_DIM, HEAD_DIM), F32)
    flat = lambda m: m.reshape(m.shape[0], m.shape[1], m.shape[2], MEM_WIDTH).astype(BF16)
    y_p, conv_p, delta_p, k_p, v_p = _run_group(
        x_prompt, flat(mem_k_p), flat(mem_v_p), conv0, delta0, None, None, 0, W, P, blocked=True)

    y_s, conv_s, delta_s, k_s, v_s = _run_group(
        x_sample, flat(cache_mem_k), flat(cache_mem_v), state_conv, state_delta, cache_k, cache_v,
        cache_k.shape[1], W, P, blocked=False)

    shape_k = lambda a, b, t: a.reshape(b, t, diff_heads, 2, HEAD_DIM)
    shape_v = lambda a, b, t: a.reshape(b, t, diff_heads, 2 * HEAD_DIM)
    return (y_p, y_s, delta_p, conv_p, shape_k(k_p, bp, tp), shape_v(v_p, bp, tp), mem_k_p, mem_v_p,
            delta_s, conv_s, shape_k(k_s, bs, ts), shape_v(v_s, bs, ts))
```

```python
import functools
import math

import jax
import jax.numpy as jnp
from jax import lax
from jax.experimental import pallas as pl
from jax.experimental.pallas import tpu as pltpu

F32 = jnp.float32
BF16 = jnp.bfloat16

EPS = 1e-6
CHUNK = 64
HEAD_DIM = 128
MEM_HEADS = 4
MEM_WIDTH = MEM_HEADS * HEAD_DIM
CONV_WIDTH = 4
N_BUCKETS = 32
MAX_DISTANCE = 128
NEG_INF = -1e30
LOG2E = math.log2(math.e)

LANES = 128
SUBLANES = 8
VMEM_LIMIT = 56 * 1024 * 1024

ROW_TILE = 512
ATTN_TILE = 512
GDN_CHUNKS = 4
GDN_HEADS = 4


def _params(*sem):
    return pltpu.CompilerParams(dimension_semantics=sem, vmem_limit_bytes=VMEM_LIMIT)


def _dot(a, b, precision=None):
    return jnp.dot(a, b, preferred_element_type=F32, precision=precision)


def _dot_nt(a, b):
    return lax.dot_general(a, b, (((1,), (1,)), ((), ())), preferred_element_type=F32)


def _dot_tn(a, b):
    return lax.dot_general(a, b, (((0,), (0,)), ((), ())), preferred_element_type=F32)


def _rms(x, gain):
    return x * lax.rsqrt(jnp.mean(x * x, axis=-1, keepdims=True) + EPS) * gain


def _silu(x):
    return x * jax.nn.sigmoid(x)


def _row_tile(rows, want):
    t = min(want, rows)
    assert rows % t == 0, (rows, t)
    return t


def _ffn_body(x_ref, g0_ref, g1_ref, wg_ref, wu_ref, wd_ref, o_ref, xn_ref, acc_ref):
    f = pl.program_id(1)

    @pl.when(f == 0)
    def _():
        xn_ref[...] = _rms(x_ref[...], g0_ref[...]).astype(BF16)

    xn = xn_ref[...]
    gate = _dot(xn, wg_ref[...])
    up = _dot(xn, wu_ref[...])
    part = _dot((_silu(gate) * up).astype(BF16), wd_ref[...])

    @pl.when(f == 0)
    def _():
        acc_ref[...] = part

    @pl.when(f != 0)
    def _():
        acc_ref[...] += part

    @pl.when(f == pl.num_programs(1) - 1)
    def _():
        o_ref[...] = x_ref[...] + 0.5 * _rms(acc_ref[...], g1_ref[...])


def _ffn(x, g0, g1, w_gu, w_down, *, tf=512):
    rows, d = x.shape
    dff = w_down.shape[0]
    tm = _row_tile(rows, ROW_TILE)
    tf = _row_tile(dff, tf)
    nf = dff // tf
    return pl.pallas_call(
        _ffn_body,
        grid=(rows // tm, nf),
        in_specs=[
            pl.BlockSpec((tm, d), lambda i, f: (i, 0)),
            pl.BlockSpec((1, d), lambda i, f: (0, 0)),
            pl.BlockSpec((1, d), lambda i, f: (0, 0)),
            pl.BlockSpec((d, tf), lambda i, f: (0, f)),
            pl.BlockSpec((d, tf), lambda i, f: (0, f + nf)),
            pl.BlockSpec((tf, d), lambda i, f: (f, 0)),
        ],
        out_specs=pl.BlockSpec((tm, d), lambda i, f: (i, 0)),
        out_shape=jax.ShapeDtypeStruct((rows, d), F32),
        scratch_shapes=[pltpu.VMEM((tm, d), BF16), pltpu.VMEM((tm, d), F32)],
        compiler_params=_params("parallel", "arbitrary"),
        name="ffn",
    )(x, g0.reshape(1, d), g1.reshape(1, d), w_gu, w_gu, w_down)


def _proj_body(n_main, has_side, *refs):
    x_ref, g_ref = refs[0], refs[1]
    w_refs = refs[2:2 + n_main]
    pos = 2 + n_main
    ws_ref = refs[pos] if has_side else None
    pos += int(has_side)
    o_refs = refs[pos:pos + n_main]
    pos += n_main
    os_ref = refs[pos] if has_side else None
    pos += int(has_side)
    xn_ref = refs[pos]

    @pl.when(pl.program_id(1) == 0)
    def _():
        xn_ref[...] = _rms(x_ref[...], g_ref[...]).astype(BF16)
        if has_side:
            os_ref[...] = _dot(xn_ref[...], ws_ref[...])

    xn = xn_ref[...]
    for w_ref, o_ref in zip(w_refs, o_refs):
        o_ref[...] = _dot(xn, w_ref[...])


def _proj(x, g, w_mains, w_side=None, *, tn=512, name="proj"):
    rows, d = x.shape
    n = w_mains[0].shape[1]
    assert all(w.shape == (d, n) for w in w_mains)
    tm = _row_tile(rows, ROW_TILE)
    tn = _row_tile(n, tn)
    has_side = w_side is not None
    in_specs = [pl.BlockSpec((tm, d), lambda i, j: (i, 0)), pl.BlockSpec((1, d), lambda i, j: (0, 0))]
    in_specs += [pl.BlockSpec((d, tn), lambda i, j: (0, j)) for _ in w_mains]
    out_specs = [pl.BlockSpec((tm, tn), lambda i, j: (i, j)) for _ in w_mains]
    out_shape = [jax.ShapeDtypeStruct((rows, n), F32) for _ in w_mains]
    args = [x, g.reshape(1, d), *w_mains]
    if has_side:
        ns = w_side.shape[1]
        in_specs.append(pl.BlockSpec((d, ns), lambda i, j: (0, 0)))
        out_specs.append(pl.BlockSpec((tm, ns), lambda i, j: (i, 0)))
        out_shape.append(jax.ShapeDtypeStruct((rows, ns), F32))
        args.append(w_side)
    return pl.pallas_call(
        functools.partial(_proj_body, len(w_mains), has_side),
        grid=(rows // tm, n // tn),
        in_specs=in_specs,
        out_specs=out_specs,
        out_shape=out_shape,
        scratch_shapes=[pltpu.VMEM((tm, d), BF16)],
        compiler_params=_params("parallel", "arbitrary"),
        name=name,
    )(*args)


def _gdn_body(nc, hg, n_heads, q_ref, k_ref, v_ref, z_ref, ba_ref, cq_ref, ck_ref, cv_ref, wq_ref, wk_ref, wv_ref,
              alog_ref, dtb_ref, onorm_ref, s0_ref, o_ref, sout_ref, xs_ref, state_ref):
    head0 = pl.program_id(1) * hg
    t = pl.program_id(2)
    tr = nc * CHUNK
    halo = SUBLANES

    @pl.when(t == 0)
    def _():
        xs_ref[0, 0:halo, :] = cq_ref[...]
        xs_ref[1, 0:halo, :] = ck_ref[...]
        xs_ref[2, 0:halo, :] = cv_ref[...]
        state_ref[...] = s0_ref[...]

    def conv(c, x_ref, w_ref):
        xs_ref[c, halo:halo + tr, :] = x_ref[...]
        w = w_ref[...]
        y = sum(xs_ref[c, halo - 3 + j:halo - 3 + j + tr, :] * w[j:j + 1, :] for j in range(CONV_WIDTH))
        xs_ref[c, 0:halo, :] = xs_ref[c, tr:tr + halo, :]
        return _silu(y)

    q_all = conv(0, q_ref, wq_ref)
    k_all = conv(1, k_ref, wk_ref)
    v_all = conv(2, v_ref, wv_ref)

    ba = ba_ref[...]
    g_all =-jnp.exp(alog_ref[...]) * jax.nn.softplus(ba + dtb_ref[...])

    row = lax.broadcasted_iota(jnp.int32, (CHUNK, 2 * CHUNK), 0)
    lane2 = lax.broadcasted_iota(jnp.int32, (CHUNK, 2 * CHUNK), 1)
    col = lane2 & (CHUNK - 1)
    left = lane2 < CHUNK
    incl = row >= col
    zeros_c = jnp.zeros((CHUNK, HEAD_DIM), F32)
    chunks = [slice(c * CHUNK, (c + 1) * CHUNK) for c in range(nc)]

    ctx = []
    for hd, sl in [(hd, sl) for hd in range(hg) for sl in chunks]:
        hs = slice(hd * HEAD_DIM, (hd + 1) * HEAD_DIM)
        qc, kc, vc = q_all[sl, hs], k_all[sl, hs], v_all[sl, hs]
        qc = qc * lax.rsqrt(jnp.sum(qc * qc, axis=-1, keepdims=True) + EPS) * (HEAD_DIM ** -0.5)
        kc = kc * lax.rsqrt(jnp.sum(kc * kc, axis=-1, keepdims=True) + EPS)
        bc = jax.nn.sigmoid(jnp.sum(jnp.where(lane2 == head0 + hd, ba[sl], 0.0), axis=1, keepdims=True))
        gc = jnp.sum(jnp.where(lane2 == head0 + hd + n_heads, g_all[sl], 0.0), axis=1, keepdims=True)
        g_row = jnp.sum(jnp.where(row == col, gc, 0.0), axis=0, keepdims=True)
        cum_col = jnp.sum(jnp.where(left, jnp.where(incl, g_row, 0.0), 0.0), axis=1, keepdims=True)
        cum_row = jnp.sum(jnp.where(row <= col, gc, 0.0), axis=0, keepdims=True)
        g_last = jnp.sum(gc, axis=0, keepdims=True)
        decay = jnp.where(incl, jnp.exp(jnp.where(incl, cum_col - cum_row, 0.0)), 0.0)
        kb = kc * bc
        kk_qk = _dot_nt(jnp.concatenate([kb, qc], axis=0).astype(BF16),
                        jnp.concatenate([kc, kc], axis=0).astype(BF16))
        lower = jnp.where(row > col, kk_qk[:CHUNK] * decay, 0.0)
        a_intra = jnp.where(incl, kk_qk[CHUNK:] * decay, 0.0)
        e_cum = jnp.exp(cum_col)
        ctx.append(dict(
            hd=hd, sl=sl, hs=hs,
            x=jnp.where(left, jnp.where(row == col, 1.0, 0.0), -lower),
            y=jnp.where(left, -lower, jnp.where(row == col, 1.0, 0.0)),
            rhs=jnp.concatenate([vc * bc, kb * e_cum], axis=1),
            a=a_intra.astype(BF16), qg=qc * e_cum,
            kd=(kc * jnp.exp(g_last - cum_col)).astype(BF16), gl=jnp.exp(g_last)))

    def split(a):
        hi = a.astype(BF16).astype(F32)
        return hi, a - hi

    for _ in range(6):
        for cx in ctx:
            xh, xl = split(cx["x"])
            yh, yl = split(cx["y"])
            lhs = jnp.concatenate([jnp.where(left, yl, xh), jnp.where(left, yh, 0.0)], axis=1)
            top = jnp.concatenate([xh, yh], axis=1)
            rhs = jnp.concatenate([top, jnp.concatenate([xl, yl], axis=1), top, jnp.zeros_like(top)], axis=0)
            prod = _dot(lhs.astype(BF16), rhs.astype(BF16))
            cx["x"] = prod[:, :2 * CHUNK] + jnp.where(left, cx["x"], 0.0)
            cx["y"] = prod[:, 2 * CHUNK:] + jnp.where(left, 0.0, cx["y"])

    for cx in ctx:
        t16 = jnp.where(left, cx["x"], 0.0).astype(BF16)
        uw = _dot(t16, jnp.concatenate([cx["rhs"], jnp.zeros_like(cx["rhs"])], axis=0).astype(BF16))
        cx["u"] = uw[:, :HEAD_DIM]
        cx["wq"] = jnp.concatenate([uw[:, HEAD_DIM:], cx["qg"]], axis=0).astype(BF16)

    states = [state_ref[hd] for hd in range(hg)]
    for cx in sorted(ctx, key=lambda cx: (cx["sl"].start, cx["hd"])):
        hd, sl, hs = cx["hd"], cx["sl"], cx["hs"]
        ws = _dot(cx["wq"], states[hd].astype(BF16))
        v_new = cx["u"] - ws[:CHUNK]
        o = ws[CHUNK:] + _dot(cx["a"], jnp.concatenate([v_new, zeros_c], axis=0).astype(BF16))
        states[hd] = states[hd] * cx["gl"] + _dot_tn(cx["kd"], v_new.astype(BF16))
        o_ref[sl, hs] = _rms(o, onorm_ref[...]) * _silu(z_ref[sl, hs])

    for hd in range(hg):
        state_ref[hd] = states[hd]
        sout_ref[hd] = states[hd]


def _gdn(proj, ba, conv_buf, conv_w, a_log, dt_bias, onorm, s0, *, nc, hg):
    bsz, tlen, _ = proj.shape
    n_heads = s0.shape[1]
    tr = nc * CHUNK
    assert tlen % tr == 0 and n_heads % hg == 0
    nt = tlen // tr
    ngroups = n_heads // hg
    width = hg * HEAD_DIM
    pad = lambda a: jnp.zeros((1, LANES), F32).at[0, n_heads:2 * n_heads].set(a)

    def col(c):
        return pl.BlockSpec((None, tr, width), lambda b, h, t: (b, t, c * ngroups + h))

    def buf(c):
        return pl.BlockSpec((None, SUBLANES, width), lambda b, h, t: (b, 0, c * ngroups + h))

    def tap(c):
        return pl.BlockSpec((CONV_WIDTH, width), lambda b, h, t: (0, c * ngroups + h))

    vec = pl.BlockSpec((1, LANES), lambda b, h, t: (0, 0))
    state_spec = pl.BlockSpec((None, hg, HEAD_DIM, HEAD_DIM), lambda b, h, t: (b, h, 0, 0))
    return pl.pallas_call(
        functools.partial(_gdn_body, nc, hg, n_heads),
        grid=(bsz, ngroups, nt),
        in_specs=[col(0), col(1), col(2), col(3),
                  pl.BlockSpec((None, tr, LANES), lambda b, h, t: (b, t, 0)),
                  buf(0), buf(1), buf(2), tap(0), tap(1), tap(2), vec, vec, vec, state_spec],
        out_specs=[pl.BlockSpec((None, tr, width), lambda b, h, t: (b, t, h)), state_spec],
        out_shape=[jax.ShapeDtypeStruct((bsz, tlen, n_heads * HEAD_DIM), F32),
                   jax.ShapeDtypeStruct(s0.shape, F32)],
        scratch_shapes=[pltpu.VMEM((3, tr + SUBLANES, width), F32), pltpu.VMEM((hg, HEAD_DIM, HEAD_DIM), F32)],
        compiler_params=_params("parallel", "parallel", "arbitrary"),
        name="gdn",
    )(proj, proj, proj, proj, ba, conv_buf, conv_buf, conv_buf, conv_w, conv_w, conv_w,
      pad(a_log), pad(dt_bias), onorm.reshape(1, HEAD_DIM), s0)


def _diff_body(causal, lam0, far_ref, q_ref, k_ref, v_ref, bias_ref, lq_ref, sub_ref, o_ref,
               qs_ref, m_ref, l_ref, acc_ref):
    i = pl.program_id(2)
    j = pl.program_id(3)
    last = i if causal else pl.num_programs(3) - 1
    tk = k_ref.shape[0]
    dk = HEAD_DIM
    row_blocks = 2 if q_ref.shape[0] % 256 == 0 else 1

    @pl.when(j == 0)
    def _():
        q = q_ref[...] * (dk ** -0.5 * LOG2E)
        qs_ref[0] = q[:, :dk].astype(BF16)
        qs_ref[1] = q[:, dk:].astype(BF16)
        m_ref[...] = jnp.full(m_ref.shape, NEG_INF, F32)
        l_ref[...] = jnp.zeros(l_ref.shape, F32)
        acc_ref[...] = jnp.zeros(acc_ref.shape, F32)

    def tile(bias, shift):
        v = v_ref[...]
        tq = q_ref.shape[0]
        rb = tq // row_blocks
        units = [(mp, slice(r * rb, (r + 1) * rb)) for r in range(row_blocks) for mp in range(2)]
        scores = [_dot_nt(qs_ref[mp, rows, :], k_ref[:, mp * dk:(mp + 1) * dk]) for mp, rows in units]
        probs = []
        for (mp, rows), s in zip(units, scores):
            if bias is not None:
                s = s + bias[rows]
            m_prev = m_ref[mp, rows, :]
            m_next = jnp.maximum(m_prev, jnp.max(s, axis=1, keepdims=True) + shift)
            p = jnp.exp2(s - jnp.concatenate([m_next - shift] * (tk // LANES), axis=1))
            alpha = jnp.exp2(m_prev - m_next)
            l_ref[mp, rows, :] = alpha * l_ref[mp, rows, :] + jnp.sum(p, axis=1, keepdims=True)
            m_ref[mp, rows, :] = m_next
            probs.append((p.astype(BF16), alpha))
        for (mp, rows), (p16, alpha) in zip(units, probs):
            acc_ref[mp, rows, :] = (acc_ref[mp, rows, :] * jnp.concatenate([alpha] * (2 * dk // LANES), axis=1)
                                    + _dot(p16, v))

    if causal:
        @pl.when(j < i - 1)
        def _():
            tile(None, far_ref[pl.program_id(1)])

        @pl.when((j >= i - 1) & (j <= i))
        def _():
            tile(bias_ref[...], 0.0)
    else:
        tile(bias_ref[...], 0.0)

    @pl.when(j == last)
    def _():
        lq = lq_ref[...]
        lam = (jnp.exp(jnp.sum(lq[0:1] * lq[1:2], axis=1, keepdims=True))
               - jnp.exp(jnp.sum(lq[2:3] * lq[3:4], axis=1, keepdims=True)) + lam0)
        reps = 2 * dk // LANES
        o0 = acc_ref[0] / jnp.concatenate([l_ref[0]] * reps, axis=1)
        o1 = acc_ref[1] / jnp.concatenate([l_ref[1]] * reps, axis=1)
        o_ref[...] = _rms(o0 - lam * o1, sub_ref[...]) * (1.0 - lam0)


def _diff_attention(q, k, v, bias, far, lam_qk, subln, lam0, *, n_heads, tq, tk, causal):
    bsz, tq_len, _ = q.shape
    tk_len = k.shape[1]
    nq, nk = tq_len // tq, tk_len // tk
    dv = 2 * HEAD_DIM
    if causal:
        assert tq == tk
        kv_map = lambda b, h, i, j: (b, jnp.minimum(j, i), h)
        bias_map = lambda b, h, i, j: (h, jnp.clip(j - i + 1, 0, 1), 0, 0)
    else:
        kv_map = lambda b, h, i, j: (b, j, h)
        bias_map = lambda b, h, i, j: (h, j, 0, 0)
    return pl.pallas_call(
        functools.partial(_diff_body, causal, lam0),
        grid=(bsz, n_heads, nq, nk),
        in_specs=[
            pl.BlockSpec(memory_space=pltpu.SMEM),
            pl.BlockSpec((None, tq, dv), lambda b, h, i, j: (b, i, h)),
            pl.BlockSpec((None, tk, dv), kv_map),
            pl.BlockSpec((None, tk, dv), kv_map),
            pl.BlockSpec((None, None, tq, tk), bias_map),
            pl.BlockSpec((4, HEAD_DIM), lambda b, h, i, j: (0, 0)),
            pl.BlockSpec((1, dv), lambda b, h, i, j: (0, 0)),
        ],
        out_specs=pl.BlockSpec((None, tq, dv), lambda b, h, i, j: (b, i, h)),
        out_shape=jax.ShapeDtypeStruct((bsz, tq_len, n_heads * dv), F32),
        scratch_shapes=[pltpu.VMEM((2, tq, HEAD_DIM), BF16), pltpu.VMEM((2, tq, LANES), F32),
                        pltpu.VMEM((2, tq, LANES), F32), pltpu.VMEM((2, tq, dv), F32)],
        compiler_params=_params("parallel", "parallel", "parallel", "arbitrary"),
        name="diff_attn",
    )(far, q, k, v, bias, lam_qk, subln.reshape(1, dv))


def _rel_bucket(rel):
    nb = N_BUCKETS // 2
    max_exact = nb // 2
    n = jnp.abs(rel)
    large = max_exact + (jnp.log(jnp.maximum(n, 1).astype(F32) / max_exact)
                         / math.log(MAX_DISTANCE / max_exact) * (nb - max_exact)).astype(jnp.int32)
    large = jnp.minimum(large, nb - 1)
    return jnp.where(rel > 0, nb, 0) + jnp.where(n < max_exact, n, large)


def _bias_window(rel_table, q_pos, k_pos):
    bucket = _rel_bucket(k_pos[None, :] - q_pos[:, None])[None]
    bias = sum(jnp.where(bucket == b, rel_table[b][:, None, None], 0.0) for b in range(N_BUCKETS))
    visible = (k_pos[None, :] // CHUNK) <= (q_pos[:, None] // CHUNK)
    return jnp.where(visible[None], bias, NEG_INF).astype(F32)


def _mix_out_body(o_ref, qm_ref, mk_ref, mv_ref, w_ref, h_ref, g_ref, out_ref, mix_ref):
    n_o = o_ref.shape[1]
    mix_ref[:, 0:n_o] = o_ref[...].astype(BF16)
    for hd in range(MEM_HEADS):
        sl = slice(hd * HEAD_DIM, (hd + 1) * HEAD_DIM)
        s = _dot_nt(qm_ref[:, sl].astype(BF16), mk_ref[:, sl]) * (HEAD_DIM ** -0.5)
        e = jnp.exp(s - jnp.max(s, axis=1, keepdims=True))
        p = e / jnp.sum(e, axis=1, keepdims=True)
        mix_ref[:, n_o + hd * HEAD_DIM:n_o + (hd + 1) * HEAD_DIM] = _dot(p.astype(BF16), mv_ref[:, sl]).astype(BF16)
    y = _dot(mix_ref[...], w_ref[...])
    out_ref[...] = h_ref[...] + _rms(y, g_ref[...])


def _mix_out(o, qm_src, qm_block, mem_k, mem_v, w_out, h, g):
    bsz, tlen, n_o = o.shape
    d = h.shape[2]
    mlen = mem_k.shape[1]
    tm = _row_tile(tlen, ROW_TILE)
    return pl.pallas_call(
        _mix_out_body,
        grid=(bsz, tlen // tm),
        in_specs=[
            pl.BlockSpec((None, tm, n_o), lambda b, i: (b, i, 0)),
            pl.BlockSpec((None, tm, MEM_WIDTH), lambda b, i: (b, i, qm_block)),
            pl.BlockSpec((None, mlen, MEM_WIDTH), lambda b, i: (b, 0, 0)),
            pl.BlockSpec((None, mlen, MEM_WIDTH), lambda b, i: (b, 0, 0)),
            pl.BlockSpec((d, d), lambda b, i: (0, 0)),
            pl.BlockSpec((None, tm, d), lambda b, i: (b, i, 0)),
            pl.BlockSpec((1, d), lambda b, i: (0, 0)),
        ],
        out_specs=pl.BlockSpec((None, tm, d), lambda b, i: (b, i, 0)),
        out_shape=jax.ShapeDtypeStruct(h.shape, F32),
        scratch_shapes=[pltpu.VMEM((tm, d), BF16)],
        compiler_params=_params("parallel", "parallel"),
        name="mix_out",
    )(o, qm_src, mem_k, mem_v, w_out, h, g.reshape(1, d))


def _lambda_init(layer_idx):
    return 0.8 - 0.6 * math.exp(-0.3 * layer_idx)


def _prep_weights(norm_gains, ffn_gate_up, ffn_down, w_out, w_mem_kv, w_in_a, conv_w_a, w_kv, w_in_b):
    d = w_out.shape[1]
    n_a = w_in_a.shape[0]
    conv_dim = conv_w_a.shape[2]
    mix = d - MEM_WIDTH
    n_heads = mix // HEAD_DIM
    qkvz = conv_dim + mix
    w = {
        "gu": ffn_gate_up.astype(BF16), "down": ffn_down.astype(BF16), "out": w_out.astype(BF16),
        "mem_k": w_mem_kv[:, :, :MEM_WIDTH].astype(BF16), "mem_v": w_mem_kv[:, :, MEM_WIDTH:].astype(BF16),
        "in_a": jnp.concatenate([w_in_a[:, :, :qkvz], w_in_a[:, :, qkvz + 2 * n_heads:]], axis=2).astype(BF16),
        "in_a_ba": jnp.pad(w_in_a[:, :, qkvz:qkvz + 2 * n_heads],
                           ((0, 0), (0, 0), (0, LANES - 2 * n_heads))).astype(BF16),
        "kv_k": w_kv[:, :mix].astype(BF16), "kv_v": w_kv[:, mix:].astype(BF16),
        "in_b": w_in_b.astype(BF16),
    }
    return w, n_a, n_heads, mix


def _run_group(x, mem_k, mem_v, conv_states, delta_states, past_k, past_v, q_pos0, W, P, *, blocked):
    w, n_a, n_heads, mix = W
    bsz, tlen, d = x.shape
    rows = bsz * tlen
    depth = P["norm_gains"].shape[0]
    diff_heads = mix // (2 * HEAD_DIM)
    h = x.reshape(rows, d)
    new_conv, new_delta = [], []
    k_new = v_new = None
    for l in range(depth):
        ng = P["norm_gains"][l]
        if l == n_a:
            k_new, v_new = _proj(h, P["kv_norm"], [w["kv_k"], w["kv_v"]], name="proj_kv")
        h = _ffn(h, ng[0], ng[1], w["gu"][l, 0], w["down"][l, 0])
        if l < n_a:
            proj, ba = _proj(h, ng[2], [w["in_a"][l]], w["in_a_ba"][l], name="proj_a")
            n_proj = proj.shape[1]
            proj = proj.reshape(bsz, tlen, n_proj)
            conv_dim = 3 * mix
            new_conv.append(proj[:, tlen - (CONV_WIDTH - 1):, :conv_dim])
            conv_buf = jnp.pad(conv_states[l], ((0, 0), (SUBLANES - (CONV_WIDTH - 1), 0), (0, 0)))
            o, s_out = _gdn(proj, ba.reshape(bsz, tlen, LANES), conv_buf, P["conv_w_a"][l], P["a_log"][l],
                            P["dt_bias"][l], P["onorm_a"][l], delta_states[l], nc=GDN_CHUNKS if tlen % (GDN_CHUNKS * CHUNK) == 0 else 1, hg=GDN_HEADS)
            new_delta.append(s_out)
            qm_src, qm_block = proj, (conv_dim + mix) // MEM_WIDTH
        else:
            jb = l - n_a
            (proj,) = _proj(h, ng[2], [w["in_b"][jb]], name="proj_b")
            proj = proj.reshape(bsz, tlen, d)
            lam0 = _lambda_init(l)
            k3 = k_new.reshape(bsz, tlen, mix).astype(BF16)
            v3 = v_new.reshape(bsz, tlen, mix).astype(BF16)
            if blocked:
                t = _row_tile(tlen, ATTN_TILE)
                pos = jnp.arange(2 * t)
                near = _bias_window(P["rel_bias"], pos[t:], pos)
                assert t >= MAX_DISTANCE
                far = P["rel_bias"][_rel_bucket(jnp.int32(-2 * MAX_DISTANCE))] * LOG2E
                bias = jnp.stack([near[:, :, :t], near[:, :, t:]], axis=1) * LOG2E
                o = _diff_attention(proj, k3, v3, bias, far, P["lambda_qk"][jb], P["subln_b"][jb], lam0,
                                    n_heads=diff_heads, tq=t, tk=t, causal=True)
            else:
                past = past_k.shape[1]
                total = past + tlen
                padded = -(-total // LANES) * LANES
                grow = lambda c, n: jnp.pad(jnp.concatenate([c.reshape(bsz, past, mix).astype(BF16), n], axis=1),
                                            ((0, 0), (0, padded - total), (0, 0)))
                k_pos = jnp.arange(padded)
                bias = _bias_window(P["rel_bias"], q_pos0 + jnp.arange(tlen), k_pos)
                bias = jnp.where(k_pos[None, None, :] < total, bias, NEG_INF)[:, None] * LOG2E
                o = _diff_attention(proj, grow(past_k, k3), grow(past_v, v3), bias, jnp.zeros((diff_heads,), F32),
                                    P["lambda_qk"][jb], P["subln_b"][jb], lam0, n_heads=diff_heads, tq=tlen,
                                    tk=padded, causal=False)
            qm_src, qm_block = proj, mix // MEM_WIDTH
        h = _mix_out(o, qm_src, qm_block, mem_k[l], mem_v[l], w["out"][l], h.reshape(bsz, tlen, d), ng[3])
        h = _ffn(h.reshape(rows, d), ng[4], ng[5], w["gu"][l, 1], w["down"][l, 1])
    return h.reshape(bsz, tlen, d), jnp.stack(new_conv), jnp.stack(new_delta), k_new, v_new


def kernel(x_prompt, x_sample, mem_prompt, cache_k, cache_v, cache_mem_k, cache_mem_v, state_delta, state_conv,
           norm_gains, ffn_gate_up, ffn_down, w_out, mem_norm, w_mem_kv, w_in_a, conv_w_a, a_log, dt_bias,
           onorm_a, kv_norm, w_kv, w_in_b, lambda_qk, subln_b, rel_bias):
    W = _prep_weights(norm_gains, ffn_gate_up, ffn_down, w_out, w_mem_kv, w_in_a, conv_w_a, w_kv, w_in_b)
    w, n_a, n_heads, mix = W
    P = {"norm_gains": norm_gains, "kv_norm": kv_norm, "conv_w_a": conv_w_a, "a_log": a_log, "dt_bias": dt_bias,
         "onorm_a": onorm_a, "lambda_qk": lambda_qk, "subln_b": subln_b, "rel_bias": rel_bias}
    depth = norm_gains.shape[0]
    bp, tp, d = x_prompt.shape
    bs, ts, _ = x_sample.shape
    mlen = mem_prompt.shape[1]
    diff_heads = mix // (2 * HEAD_DIM)

    mem_rows = mem_prompt.reshape(bp * mlen, d)
    mem_kv = [_proj(mem_rows, mem_norm[l], [w["mem_k"][l], w["mem_v"][l]], name="proj_mem") for l in range(depth)]
    mem_k_p = jnp.stack([kv[0] for kv in mem_kv]).reshape(depth, bp, mlen, MEM_HEADS, HEAD_DIM)
    mem_v_p = jnp.stack([kv[1] for kv in mem_kv]).reshape(depth, bp, mlen, MEM_HEADS, HEAD_DIM)
    conv0 = jnp.zeros((n_a, bp, CONV_WIDTH - 1, 3 * mix), F32)
    delta0 = jnp.zeros((n_a, bp, n_heads, HEAD_DIM, HEAD_DIM), F32)
    flat = lambda m: m.reshape(m.shape[0], m.shape[1], m.shape[2], MEM_WIDTH).astype(BF16)
    y_p, conv_p, delta_p, k_p, v_p = _run_group(
        x_prompt, flat(mem_k_p), flat(mem_v_p), conv0, delta0, None, None, 0, W, P, blocked=True)

    y_s, conv_s, delta_s, k_s, v_s = _run_group(
        x_sample, flat(cache_mem_k), flat(cache_mem_v), state_conv, state_delta, cache_k, cache_v,
        cache_k.shape[1], W, P, blocked=False)

    shape_k = lambda a, b, t: a.reshape(b, t, diff_heads, 2, HEAD_DIM)
    shape_v = lambda a, b, t: a.reshape(b, t, diff_heads, 2 * HEAD_DIM)
    return (y_p, y_s, delta_p, conv_p, shape_k(k_p, bp, tp), shape_v(v_p, bp, tp), mem_k_p, mem_v_p,
            delta_s, conv_s, shape_k(k_s, bs, ts), shape_v(v_s, bs, ts))
```

```python
import functools
import math

import jax
import jax.numpy as jnp
from jax import lax
from jax.experimental import pallas as pl
from jax.experimental.pallas import tpu as pltpu

F32 = jnp.float32
BF16 = jnp.bfloat16

EPS = 1e-6
CHUNK = 64
HEAD_DIM = 128
MEM_HEADS = 4
MEM_WIDTH = MEM_HEADS * HEAD_DIM
CONV_WIDTH = 4
N_BUCKETS = 32
MAX_DISTANCE = 128
NEG_INF = -1e30
LOG2E = math.log2(math.e)

LANES = 128
SUBLANES = 8
VMEM_LIMIT = 56 * 1024 * 1024

ROW_TILE = 512
FFN_ROW_TILE = 1024
FFN_VMEM_LIMIT = 60 * 1024 * 1024
ATTN_TILE = 512
GDN_CHUNKS = 4
GDN_HEADS = 4


def _params(*sem):
    return pltpu.CompilerParams(dimension_semantics=sem, vmem_limit_bytes=VMEM_LIMIT)


def _dot(a, b, precision=None):
    return jnp.dot(a, b, preferred_element_type=F32, precision=precision)


def _dot_nt(a, b):
    return lax.dot_general(a, b, (((1,), (1,)), ((), ())), preferred_element_type=F32)


def _dot_tn(a, b):
    return lax.dot_general(a, b, (((0,), (0,)), ((), ())), preferred_element_type=F32)


def _rms(x, gain):
    return x * lax.rsqrt(jnp.mean(x * x, axis=-1, keepdims=True) + EPS) * gain


def _silu(x):
    return x * jax.nn.sigmoid(x)


def _row_tile(rows, want):
    t = min(want, rows)
    assert rows % t == 0, (rows, t)
    return t


def _ffn_body(x_ref, g0_ref, g1_ref, wg_ref, wu_ref, wd_ref, o_ref, xn_ref):
    f = pl.program_id(1)

    @pl.when(f == 0)
    def _():
        xn_ref[...] = _rms(x_ref[...], g0_ref[...]).astype(BF16)
        o_ref[...] = jnp.zeros(o_ref.shape, F32)

    xn = xn_ref[...]
    gate = _dot(xn, wg_ref[...])
    up = _dot(xn, wu_ref[...])
    o_ref[...] += _dot((_silu(gate) * up).astype(BF16), wd_ref[...])

    @pl.when(f == pl.num_programs(1) - 1)
    def _():
        o_ref[...] = x_ref[...] + 0.5 * _rms(o_ref[...], g1_ref[...])


def _ffn(x, g0, g1, w_gu, w_down, layer, half, *, tf=512):
    rows, d = x.shape
    dff = w_down.shape[2]
    tm = _row_tile(rows, FFN_ROW_TILE)
    tf = _row_tile(dff, tf)
    nf = dff // tf
    return pl.pallas_call(
        _ffn_body,
        grid=(rows // tm, nf),
        in_specs=[
            pl.BlockSpec((tm, d), lambda i, f: (i, 0), pipeline_mode=pl.Buffered(1)),
            pl.BlockSpec((1, d), lambda i, f: (0, 0)),
            pl.BlockSpec((1, d), lambda i, f: (0, 0)),
            pl.BlockSpec((None, None, d, tf), lambda i, f: (layer, half, 0, f)),
            pl.BlockSpec((None, None, d, tf), lambda i, f: (layer, half, 0, f + nf)),
            pl.BlockSpec((None, None, tf, d), lambda i, f: (layer, half, f, 0)),
        ],
        out_specs=pl.BlockSpec((tm, d), lambda i, f: (i, 0)),
        out_shape=jax.ShapeDtypeStruct((rows, d), F32),
        scratch_shapes=[pltpu.VMEM((tm, d), BF16)],
        compiler_params=pltpu.CompilerParams(dimension_semantics=("parallel", "arbitrary"),
                                             vmem_limit_bytes=FFN_VMEM_LIMIT),
        name="ffn",
    )(x, g0.reshape(1, d), g1.reshape(1, d), w_gu, w_gu, w_down)


def _proj_body(n_main, has_side, with_bf16, *refs):
    x_ref, g_ref = refs[0], refs[1]
    w_refs = refs[2:2 + n_main]
    pos = 2 + n_main
    ws_ref = refs[pos] if has_side else None
    pos += int(has_side)
    o_refs = refs[pos:pos + n_main]
    pos += n_main
    o16_refs = refs[pos:pos + n_main] if with_bf16 else [None] * n_main
    pos += n_main * int(with_bf16)
    os_ref = refs[pos] if has_side else None
    pos += int(has_side)
    xn_ref = refs[pos]

    @pl.when(pl.program_id(1) == 0)
    def _():
        xn_ref[...] = _rms(x_ref[...], g_ref[...]).astype(BF16)
        if has_side:
            os_ref[...] = _dot(xn_ref[...], ws_ref[...])

    xn = xn_ref[...]
    for w_ref, o_ref, o16_ref in zip(w_refs, o_refs, o16_refs):
        y = _dot(xn, w_ref[...])
        if len(o_ref.shape) == 2:
            o_ref[...] = y
        else:
            _, nh, nm, dh = o_ref.shape
            for hh in range(nh):
                for mm in range(nm):
                    c0 = (hh * nm + mm) * dh
                    o_ref[:, hh, mm, :] = y[:, c0:c0 + dh]
        if with_bf16:
            o16_ref[...] = y.astype(BF16)


def _proj(x, g, w_mains, w_side=None, *, tm, tn, with_bf16=False, split_first=None, name="proj"):
    rows, d = x.shape
    n = w_mains[0].shape[1]
    assert all(w.shape == (d, n) for w in w_mains)
    tm = _row_tile(rows, tm)
    tn = _row_tile(n, tn)
    has_side = w_side is not None
    in_specs = [pl.BlockSpec((tm, d), lambda i, j: (i, 0)), pl.BlockSpec((1, d), lambda i, j: (0, 0))]
    in_specs += [pl.BlockSpec((d, tn), lambda i, j: (0, j)) for _ in w_mains]
    out_specs = [pl.BlockSpec((tm, tn), lambda i, j: (i, j)) for _ in w_mains]
    out_shape = [jax.ShapeDtypeStruct((rows, n), F32) for _ in w_mains]
    if split_first is not None:
        nm = split_first
        group = nm * HEAD_DIM
        out_specs[0] = pl.BlockSpec((tm, tn // group, nm, HEAD_DIM), lambda i, j: (i, j, 0, 0))
        out_shape[0] = jax.ShapeDtypeStruct((rows, n // group, nm, HEAD_DIM), F32)
    if with_bf16:
        out_specs += [pl.BlockSpec((tm, tn), lambda i, j: (i, j)) for _ in w_mains]
        out_shape += [jax.ShapeDtypeStruct((rows, n), BF16) for _ in w_mains]
    args = [x, g.reshape(1, d), *w_mains]
    if has_side:
        ns = w_side.shape[1]
        in_specs.append(pl.BlockSpec((d, ns), lambda i, j: (0, 0)))
        out_specs.append(pl.BlockSpec((tm, ns), lambda i, j: (i, 0)))
        out_shape.append(jax.ShapeDtypeStruct((rows, ns), F32))
        args.append(w_side)
    return pl.pallas_call(
        functools.partial(_proj_body, len(w_mains), has_side, with_bf16),
        grid=(rows // tm, n // tn),
        in_specs=in_specs,
        out_specs=out_specs,
        out_shape=out_shape,
        scratch_shapes=[pltpu.VMEM((tm, d), BF16)],
        compiler_params=_params("parallel", "arbitrary"),
        name=name,
    )(*args)


def _gdn_body(nc, hg, n_heads, q_ref, k_ref, v_ref, z_ref, ba_ref, cq_ref, ck_ref, cv_ref, wq_ref, wk_ref, wv_ref,
              alog_ref, dtb_ref, onorm_ref, s0_ref, o_ref, sout_ref, xs_ref, state_ref):
    head0 = pl.program_id(1) * hg
    t = pl.program_id(2)
    tr = nc * CHUNK
    halo = SUBLANES

    @pl.when(t == 0)
    def _():
        xs_ref[0, 0:halo, :] = cq_ref[...]
        xs_ref[1, 0:halo, :] = ck_ref[...]
        xs_ref[2, 0:halo, :] = cv_ref[...]
        state_ref[...] = s0_ref[...]

    def conv(c, x_ref, w_ref):
        xs_ref[c, halo:halo + tr, :] = x_ref[...]
        w = w_ref[...]
        y = sum(xs_ref[c, halo - 3 + j:halo - 3 + j + tr, :] * w[j:j + 1, :] for j in range(CONV_WIDTH))
        xs_ref[c, 0:halo, :] = xs_ref[c, tr:tr + halo, :]
        return _silu(y)

    q_all = conv(0, q_ref, wq_ref)
    k_all = conv(1, k_ref, wk_ref)
    v_all = conv(2, v_ref, wv_ref)

    ba = ba_ref[...]
    g_all =-jnp.exp(alog_ref[...]) * jax.nn.softplus(ba + dtb_ref[...])

    row = lax.broadcasted_iota(jnp.int32, (CHUNK, 2 * CHUNK), 0)
    lane2 = lax.broadcasted_iota(jnp.int32, (CHUNK, 2 * CHUNK), 1)
    col = lane2 & (CHUNK - 1)
    left = lane2 < CHUNK
    incl = row >= col
    zeros_c = jnp.zeros((CHUNK, HEAD_DIM), F32)
    chunks = [slice(c * CHUNK, (c + 1) * CHUNK) for c in range(nc)]

    ctx = []
    for hd, sl in [(hd, sl) for hd in range(hg) for sl in chunks]:
        hs = slice(hd * HEAD_DIM, (hd + 1) * HEAD_DIM)
        qc, kc, vc = q_all[sl, hs], k_all[sl, hs], v_all[sl, hs]
        qc = qc * lax.rsqrt(jnp.sum(qc * qc, axis=-1, keepdims=True) + EPS) * (HEAD_DIM ** -0.5)
        kc = kc * lax.rsqrt(jnp.sum(kc * kc, axis=-1, keepdims=True) + EPS)
        bc = jax.nn.sigmoid(jnp.sum(jnp.where(lane2 == head0 + hd, ba[sl], 0.0), axis=1, keepdims=True))
        gc = jnp.sum(jnp.where(lane2 == head0 + hd + n_heads, g_all[sl], 0.0), axis=1, keepdims=True)
        g_row = jnp.sum(jnp.where(row == col, gc, 0.0), axis=0, keepdims=True)
        cum_col = jnp.sum(jnp.where(left, jnp.where(incl, g_row, 0.0), 0.0), axis=1, keepdims=True)
        cum_row = jnp.sum(jnp.where(row <= col, gc, 0.0), axis=0, keepdims=True)
        g_last = jnp.sum(gc, axis=0, keepdims=True)
        decay = jnp.where(incl, jnp.exp(jnp.where(incl, cum_col - cum_row, 0.0)), 0.0)
        kb = kc * bc
        kk_qk = _dot_nt(jnp.concatenate([kb, qc], axis=0).astype(BF16),
                        jnp.concatenate([kc, kc], axis=0).astype(BF16))
        lower = jnp.where(row > col, kk_qk[:CHUNK] * decay, 0.0)
        a_intra = jnp.where(incl, kk_qk[CHUNK:] * decay, 0.0)
        e_cum = jnp.exp(cum_col)
        ctx.append(dict(
            hd=hd, sl=sl, hs=hs,
            x=jnp.where(left, jnp.where(row == col, 1.0, 0.0), -lower),
            y=jnp.where(left, -lower, jnp.where(row == col, 1.0, 0.0)),
            rhs=jnp.concatenate([vc * bc, kb * e_cum], axis=1),
            a=a_intra.astype(BF16), qg=qc * e_cum,
            kd=(kc * jnp.exp(g_last - cum_col)).astype(BF16), gl=jnp.exp(g_last)))

    def split(a):
        hi = a.astype(BF16).astype(F32)
        return hi, a - hi

    for _ in range(6):
        for cx in ctx:
            xh, xl = split(cx["x"])
            yh, yl = split(cx["y"])
            lhs = jnp.concatenate([jnp.where(left, yl, xh), jnp.where(left, yh, 0.0)], axis=1)
            top = jnp.concatenate([xh, yh], axis=1)
            rhs = jnp.concatenate([top, jnp.concatenate([xl, yl], axis=1), top, jnp.zeros_like(top)], axis=0)
            prod = _dot(lhs.astype(BF16), rhs.astype(BF16))
            cx["x"] = prod[:, :2 * CHUNK] + jnp.where(left, cx["x"], 0.0)
            cx["y"] = prod[:, 2 * CHUNK:] + jnp.where(left, 0.0, cx["y"])

    for cx in ctx:
        t16 = jnp.where(left, cx["x"], 0.0).astype(BF16)
        uw = _dot(t16, jnp.concatenate([cx["rhs"], jnp.zeros_like(cx["rhs"])], axis=0).astype(BF16))
        cx["u"] = uw[:, :HEAD_DIM]
        cx["wq"] = jnp.concatenate([uw[:, HEAD_DIM:], cx["qg"]], axis=0).astype(BF16)

    states = [state_ref[hd] for hd in range(hg)]
    for cx in sorted(ctx, key=lambda cx: (cx["sl"].start, cx["hd"])):
        hd, sl, hs = cx["hd"], cx["sl"], cx["hs"]
        ws = _dot(cx["wq"], states[hd].astype(BF16))
        v_new = cx["u"] - ws[:CHUNK]
        o = ws[CHUNK:] + _dot(cx["a"], jnp.concatenate([v_new, zeros_c], axis=0).astype(BF16))
        states[hd] = states[hd] * cx["gl"] + _dot_tn(cx["kd"], v_new.astype(BF16))
        o_ref[sl, hs] = _rms(o, onorm_ref[...]) * _silu(z_ref[sl, hs])

    for hd in range(hg):
        state_ref[hd] = states[hd]
        sout_ref[hd] = states[hd]


def _gdn(proj, ba, conv_buf, conv_w, a_log, dt_bias, onorm, s0, *, nc, hg):
    bsz, tlen, _ = proj.shape
    n_heads = s0.shape[1]
    tr = nc * CHUNK
    assert tlen % tr == 0 and n_heads % hg == 0
    nt = tlen // tr
    ngroups = n_heads // hg
    width = hg * HEAD_DIM
    pad = lambda a: jnp.zeros((1, LANES), F32).at[0, n_heads:2 * n_heads].set(a)

    def col(c):
        return pl.BlockSpec((None, tr, width), lambda b, h, t: (b, t, c * ngroups + h))

    def buf(c):
        return pl.BlockSpec((None, SUBLANES, width), lambda b, h, t: (b, 0, c * ngroups + h))

    def tap(c):
        return pl.BlockSpec((CONV_WIDTH, width), lambda b, h, t: (0, c * ngroups + h))

    vec = pl.BlockSpec((1, LANES), lambda b, h, t: (0, 0))
    state_spec = pl.BlockSpec((None, hg, HEAD_DIM, HEAD_DIM), lambda b, h, t: (b, h, 0, 0))
    return pl.pallas_call(
        functools.partial(_gdn_body, nc, hg, n_heads),
        grid=(bsz, ngroups, nt),
        in_specs=[col(0), col(1), col(2), col(3),
                  pl.BlockSpec((None, tr, LANES), lambda b, h, t: (b, t, 0)),
                  buf(0), buf(1), buf(2), tap(0), tap(1), tap(2), vec, vec, vec, state_spec],
        out_specs=[pl.BlockSpec((None, tr, width), lambda b, h, t: (b, t, h)), state_spec],
        out_shape=[jax.ShapeDtypeStruct((bsz, tlen, n_heads * HEAD_DIM), F32),
                   jax.ShapeDtypeStruct(s0.shape, F32)],
        scratch_shapes=[pltpu.VMEM((3, tr + SUBLANES, width), F32), pltpu.VMEM((hg, HEAD_DIM, HEAD_DIM), F32)],
        compiler_params=_params("parallel", "parallel", "arbitrary"),
        name="gdn",
    )(proj, proj, proj, proj, ba, conv_buf, conv_buf, conv_buf, conv_w, conv_w, conv_w,
      pad(a_log), pad(dt_bias), onorm.reshape(1, HEAD_DIM), s0)


def _diff_body(causal, lam0, far_ref, q_ref, k_ref, v_ref, bias_ref, lq_ref, sub_ref, o_ref,
               qs_ref, s_ref, m_ref, l_ref, acc_ref):
    i = pl.program_id(2)
    j = pl.program_id(3)
    last = i if causal else pl.num_programs(3) - 2
    tk = k_ref.shape[0]
    dk = HEAD_DIM
    tq = q_ref.shape[0]
    row_blocks = 2 if tq % 256 == 0 else 1
    rb = tq // row_blocks
    units = [(mp, slice(r * rb, (r + 1) * rb)) for r in range(row_blocks) for mp in range(2)]

    @pl.when(j == 0)
    def _():
        q = q_ref[...] * (dk ** -0.5 * LOG2E)
        qs_ref[0] = q[:, :dk].astype(BF16)
        qs_ref[1] = q[:, dk:].astype(BF16)
        m_ref[...] = jnp.full(m_ref.shape, NEG_INF, F32)
        l_ref[...] = jnp.zeros(l_ref.shape, F32)
        acc_ref[...] = jnp.zeros(acc_ref.shape, F32)

    def scores(slot):
        for mp, rows in units:
            s_ref[slot, mp, rows, :] = _dot_nt(qs_ref[mp, rows, :], k_ref[:, mp * dk:(mp + 1) * dk])

    def update(slot, bias, shift):
        v = v_ref[...]
        probs = []
        for mp, rows in units:
            s = s_ref[slot, mp, rows, :]
            if bias is not None:
                s = s + bias[rows]
            m_prev = m_ref[mp, rows, :]
            m_next = jnp.maximum(m_prev, jnp.max(s, axis=1, keepdims=True) + shift)
            p = jnp.exp2(s - jnp.concatenate([m_next - shift] * (tk // LANES), axis=1))
            alpha = jnp.exp2(m_prev - m_next)
            l_ref[mp, rows, :] = alpha * l_ref[mp, rows, :] + jnp.sum(p, axis=1, keepdims=True)
            m_ref[mp, rows, :] = m_next
            probs.append((p.astype(BF16), alpha))
        for (mp, rows), (p16, alpha) in zip(units, probs):
            acc_ref[mp, rows, :] = (acc_ref[mp, rows, :] * jnp.concatenate([alpha] * (2 * dk // LANES), axis=1)
                                    + _dot(p16, v))

    @pl.when(j == 0)
    def _():
        scores(0)

    for cur in range(2):
        prev = 1 - cur
        par = (j % 2) == cur
        if causal:
            @pl.when(par & (j >= 1) & (j < i))
            def _():
                scores(cur)
                update(prev, None, far_ref[pl.program_id(1)])

            @pl.when(par & (j >= 1) & (j == i))
            def _():
                scores(cur)
                update(prev, bias_ref[...], 0.0)
        else:
            @pl.when(par & (j >= 1) & (j <= last))
            def _():
                scores(cur)
                update(prev, bias_ref[...], 0.0)

        @pl.when(par & (j == last + 1))
        def _():
            update(prev, bias_ref[...], 0.0)

    @pl.when(j == last + 1)
    def _():
        lq = lq_ref[...]
        lam = (jnp.exp(jnp.sum(lq[0:1] * lq[1:2], axis=1, keepdims=True))
               - jnp.exp(jnp.sum(lq[2:3] * lq[3:4], axis=1, keepdims=True)) + lam0)
        reps = 2 * dk // LANES
        o0 = acc_ref[0] / jnp.concatenate([l_ref[0]] * reps, axis=1)
        o1 = acc_ref[1] / jnp.concatenate([l_ref[1]] * reps, axis=1)
        o_ref[...] = _rms(o0 - lam * o1, sub_ref[...]) * (1.0 - lam0)


def _diff_attention(q, k, v, bias, far, lam_qk, subln, lam0, *, n_heads, tq, tk, causal):
    bsz, tq_len, _ = q.shape
    tk_len = k.shape[1]
    nq, nk = tq_len // tq, tk_len // tk
    dv = 2 * HEAD_DIM
    if causal:
        assert tq == tk
        k_map = lambda b, h, i, j: (b, jnp.minimum(j, i), h)
        v_map = lambda b, h, i, j: (b, jnp.clip(j - 1, 0, i), h)
        bias_map = lambda b, h, i, j: (h, jnp.clip(j - i, 0, 1), 0, 0)
    else:
        k_map = lambda b, h, i, j: (b, jnp.minimum(j, nk - 1), h)
        v_map = lambda b, h, i, j: (b, jnp.maximum(j - 1, 0), h)
        bias_map = lambda b, h, i, j: (h, jnp.maximum(j - 1, 0), 0, 0)
    return pl.pallas_call(
        functools.partial(_diff_body, causal, lam0),
        grid=(bsz, n_heads, nq, nk + 1),
        in_specs=[
            pl.BlockSpec(memory_space=pltpu.SMEM),
            pl.BlockSpec((None, tq, dv), lambda b, h, i, j: (b, i, h)),
            pl.BlockSpec((None, tk, dv), k_map),
            pl.BlockSpec((None, tk, dv), v_map),
            pl.BlockSpec((None, None, tq, tk), bias_map),
            pl.BlockSpec((4, HEAD_DIM), lambda b, h, i, j: (0, 0)),
            pl.BlockSpec((1, dv), lambda b, h, i, j: (0, 0)),
        ],
        out_specs=pl.BlockSpec((None, tq, dv), lambda b, h, i, j: (b, i, h)),
        out_shape=jax.ShapeDtypeStruct((bsz, tq_len, n_heads * dv), F32),
        scratch_shapes=[pltpu.VMEM((2, tq, HEAD_DIM), BF16), pltpu.VMEM((2, 2, tq, tk), F32),
                        pltpu.VMEM((2, tq, LANES), F32), pltpu.VMEM((2, tq, LANES), F32),
                        pltpu.VMEM((2, tq, dv), F32)],
        compiler_params=_params("parallel", "parallel", "parallel", "arbitrary"),
        name="diff_attn",
    )(far, q, k, v, bias, lam_qk, subln.reshape(1, dv))


def _rel_bucket(rel):
    nb = N_BUCKETS // 2
    max_exact = nb // 2
    n = jnp.abs(rel)
    large = max_exact + (jnp.log(jnp.maximum(n, 1).astype(F32) / max_exact)
                         / math.log(MAX_DISTANCE / max_exact) * (nb - max_exact)).astype(jnp.int32)
    large = jnp.minimum(large, nb - 1)
    return jnp.where(rel > 0, nb, 0) + jnp.where(n < max_exact, n, large)


def _bias_window(rel_table, q_pos, k_pos):
    bucket = _rel_bucket(k_pos[None, :] - q_pos[:, None])[None]
    bias = sum(jnp.where(bucket == b, rel_table[b][:, None, None], 0.0) for b in range(N_BUCKETS))
    visible = (k_pos[None, :] // CHUNK) <= (q_pos[:, None] // CHUNK)
    return jnp.where(visible[None], bias, NEG_INF).astype(F32)


def _mix_out_body(o_ref, qm_ref, mk_ref, mv_ref, w_ref, h_ref, g_ref, out_ref, mix_ref):
    n_o = o_ref.shape[1]
    mix_ref[:, 0:n_o] = o_ref[...].astype(BF16)
    for hd in range(MEM_HEADS):
        sl = slice(hd * HEAD_DIM, (hd + 1) * HEAD_DIM)
        s = _dot_nt(qm_ref[:, sl].astype(BF16), mk_ref[:, sl]) * (HEAD_DIM ** -0.5)
        e = jnp.exp(s - jnp.max(s, axis=1, keepdims=True))
        p = e / jnp.sum(e, axis=1, keepdims=True)
        mix_ref[:, n_o + hd * HEAD_DIM:n_o + (hd + 1) * HEAD_DIM] = _dot(p.astype(BF16), mv_ref[:, sl]).astype(BF16)
    y = _dot(mix_ref[...], w_ref[...])
    out_ref[...] = h_ref[...] + _rms(y, g_ref[...])


def _mix_out(o, qm_src, qm_block, mem_k, mem_v, w_out, h, g):
    bsz, tlen, n_o = o.shape
    d = h.shape[2]
    mlen = mem_k.shape[1]
    tm = _row_tile(tlen, ROW_TILE)
    return pl.pallas_call(
        _mix_out_body,
        grid=(bsz, tlen // tm),
        in_specs=[
            pl.BlockSpec((None, tm, n_o), lambda b, i: (b, i, 0)),
            pl.BlockSpec((None, tm, MEM_WIDTH), lambda b, i: (b, i, qm_block)),
            pl.BlockSpec((None, mlen, MEM_WIDTH), lambda b, i: (b, 0, 0)),
            pl.BlockSpec((None, mlen, MEM_WIDTH), lambda b, i: (b, 0, 0)),
            pl.BlockSpec((d, d), lambda b, i: (0, 0)),
            pl.BlockSpec((None, tm, d), lambda b, i: (b, i, 0)),
            pl.BlockSpec((1, d), lambda b, i: (0, 0)),
        ],
        out_specs=pl.BlockSpec((None, tm, d), lambda b, i: (b, i, 0)),
        out_shape=jax.ShapeDtypeStruct(h.shape, F32),
        scratch_shapes=[pltpu.VMEM((tm, d), BF16)],
        compiler_params=_params("parallel", "parallel"),
        name="mix_out",
    )(o, qm_src, mem_k, mem_v, w_out, h, g.reshape(1, d))


def _lambda_init(layer_idx):
    return 0.8 - 0.6 * math.exp(-0.3 * layer_idx)


def _prep_weights(norm_gains, ffn_gate_up, ffn_down, w_out, w_mem_kv, w_in_a, conv_w_a, w_kv, w_in_b):
    d = w_out.shape[1]
    n_a = w_in_a.shape[0]
    conv_dim = conv_w_a.shape[2]
    mix = d - MEM_WIDTH
    n_heads = mix // HEAD_DIM
    qkvz = conv_dim + mix
    w = {
        "gu": ffn_gate_up.astype(BF16), "down": ffn_down.astype(BF16), "out": w_out.astype(BF16),
        "mem_k": w_mem_kv[:, :, :MEM_WIDTH].astype(BF16), "mem_v": w_mem_kv[:, :, MEM_WIDTH:].astype(BF16),
        "in_a": jnp.concatenate([w_in_a[:, :, :qkvz], w_in_a[:, :, qkvz + 2 * n_heads:]], axis=2).astype(BF16),
        "in_a_ba": jnp.pad(w_in_a[:, :, qkvz:qkvz + 2 * n_heads],
                           ((0, 0), (0, 0), (0, LANES - 2 * n_heads))).astype(BF16),
        "kv_k": w_kv[:, :mix].astype(BF16), "kv_v": w_kv[:, mix:].astype(BF16),
        "in_b": w_in_b.astype(BF16),
    }
    return w, n_a, n_heads, mix


def _run_group(x, mem_k, mem_v, conv_states, delta_states, past_k, past_v, q_pos0, W, P, *, blocked):
    w, n_a, n_heads, mix = W
    bsz, tlen, d = x.shape
    rows = bsz * tlen
    depth = P["norm_gains"].shape[0]
    diff_heads = mix // (2 * HEAD_DIM)
    h = x.reshape(rows, d)
    new_conv, new_delta = [], []
    k_new = v_new = None
    for l in range(depth):
        ng = P["norm_gains"][l]
        if l == n_a:
            k_new, v_new, k16, v16 = _proj(h, P["kv_norm"], [w["kv_k"], w["kv_v"]], tm=512, tn=768,
                                           with_bf16=True, split_first=2, name="proj_kv")
        h = _ffn(h, ng[0], ng[1], w["gu"], w["down"], l, 0)
        if l < n_a:
            proj, ba = _proj(h, ng[2], [w["in_a"][l]], w["in_a_ba"][l], tm=1024, tn=1664, name="proj_a")
            n_proj = proj.shape[1]
            proj = proj.reshape(bsz, tlen, n_proj)
            conv_dim = 3 * mix
            new_conv.append(proj[:, tlen - (CONV_WIDTH - 1):, :conv_dim])
            conv_buf = jnp.pad(conv_states[l], ((0, 0), (SUBLANES - (CONV_WIDTH - 1), 0), (0, 0)))
            o, s_out = _gdn(proj, ba.reshape(bsz, tlen, LANES), conv_buf, P["conv_w_a"][l], P["a_log"][l],
                            P["dt_bias"][l], P["onorm_a"][l], delta_states[l], nc=GDN_CHUNKS if tlen % (GDN_CHUNKS * CHUNK) == 0 else 1, hg=GDN_HEADS)
            new_delta.append(s_out)
            qm_src, qm_block = proj, (conv_dim + mix) // MEM_WIDTH
        else:
            jb = l - n_a
            (proj,) = _proj(h, ng[2], [w["in_b"][jb]], tm=512, tn=2048, name="proj_b")
            proj = proj.reshape(bsz, tlen, d)
            lam0 = _lambda_init(l)
            k3 = k16.reshape(bsz, tlen, mix)
            v3 = v16.reshape(bsz, tlen, mix)
            if blocked:
                t = _row_tile(tlen, ATTN_TILE)
                pos = jnp.arange(2 * t)
                near = _bias_window(P["rel_bias"], pos[t:], pos)
                assert t >= MAX_DISTANCE
                far = P["rel_bias"][_rel_bucket(jnp.int32(-2 * MAX_DISTANCE))] * LOG2E
                bias = jnp.stack([near[:, :, :t], near[:, :, t:]], axis=1) * LOG2E
                o = _diff_attention(proj, k3, v3, bias, far, P["lambda_qk"][jb], P["subln_b"][jb], lam0,
                                    n_heads=diff_heads, tq=t, tk=t, causal=True)
            else:
                past = past_k.shape[1]
                total = past + tlen
                padded = -(-total // LANES) * LANES
                grow = lambda c, n: jnp.pad(jnp.concatenate([c.reshape(bsz, past, mix).astype(BF16), n], axis=1),
                                            ((0, 0), (0, padded - total), (0, 0)))
                k_pos = jnp.arange(padded)
                bias = _bias_window(P["rel_bias"], q_pos0 + jnp.arange(tlen), k_pos)
                bias = jnp.where(k_pos[None, None, :] < total, bias, NEG_INF)[:, None] * LOG2E
                o = _diff_attention(proj, grow(past_k, k3), grow(past_v, v3), bias, jnp.zeros((diff_heads,), F32),
                                    P["lambda_qk"][jb], P["subln_b"][jb], lam0, n_heads=diff_heads, tq=tlen,
                                    tk=padded, causal=False)
            qm_src, qm_block = proj, mix // MEM_WIDTH
        h = _mix_out(o, qm_src, qm_block, mem_k[l], mem_v[l], w["out"][l], h.reshape(bsz, tlen, d), ng[3])
        h = _ffn(h.reshape(rows, d), ng[4], ng[5], w["gu"], w["down"], l, 1)
    return h.reshape(bsz, tlen, d), jnp.stack(new_conv), jnp.stack(new_delta), k_new, v_new


def kernel(x_prompt, x_sample, mem_prompt, cache_k, cache_v, cache_mem_k, cache_mem_v, state_delta, state_conv,
           norm_gains, ffn_gate_up, ffn_down, w_out, mem_norm, w_mem_kv, w_in_a, conv_w_a, a_log, dt_bias,
           onorm_a, kv_norm, w_kv, w_in_b, lambda_qk, subln_b, rel_bias):
    W = _prep_weights(norm_gains, ffn_gate_up, ffn_down, w_out, w_mem_kv, w_in_a, conv_w_a, w_kv, w_in_b)
    w, n_a, n_heads, mix = W
    P = {"norm_gains": norm_gains, "kv_norm": kv_norm, "conv_w_a": conv_w_a, "a_log": a_log, "dt_bias": dt_bias,
         "onorm_a": onorm_a, "lambda_qk": lambda_qk, "subln_b": subln_b, "rel_bias": rel_bias}
    depth = norm_gains.shape[0]
    bp, tp, d = x_prompt.shape
    bs, ts, _ = x_sample.shape
    mlen = mem_prompt.shape[1]
    diff_heads = mix // (2 * HEAD_DIM)

    mem_rows = mem_prompt.reshape(bp * mlen, d)
    mem_kv = [_proj(mem_rows, mem_norm[l], [w["mem_k"][l], w["mem_v"][l]], tm=256, tn=512, name="proj_mem")
              for l in range(depth)]
    mem_k_p = jnp.stack([kv[0] for kv in mem_kv]).reshape(depth, bp, mlen, MEM_HEADS, HEAD_DIM)
    mem_v_p = jnp.stack([kv[1] for kv in mem_kv]).reshape(depth, bp, mlen, MEM_HEADS, HEAD_DIM)
    conv0 = jnp.zeros((n_a, bp, CONV_WIDTH - 1, 3 * mix), F32)
    delta0 = jnp.zeros((n_a, bp, n_heads, HEAD_DIM, HEAD_DIM), F32)
    flat = lambda m: m.reshape(m.shape[0], m.shape[1], m.shape[2], MEM_WIDTH).astype(BF16)
    y_p, conv_p, delta_p, k_p, v_p = _run_group(
        x_prompt, flat(mem_k_p), flat(mem_v_p), conv0, delta0, None, None, 0, W, P, blocked=True)

    y_s, conv_s, delta_s, k_s, v_s = _run_group(
        x_sample, flat(cache_mem_k), flat(cache_mem_v), state_conv, state_delta, cache_k, cache_v,
        cache_k.shape[1], W, P, blocked=False)

    shape_k = lambda a, b, t: a.reshape(b, t, diff_heads, 2, HEAD_DIM)
    shape_v = lambda a, b, t: a.reshape(b, t, diff_heads, 2 * HEAD_DIM)
    return (y_p, y_s, delta_p, conv_p, shape_k(k_p, bp, tp), shape_v(v_p, bp, tp), mem_k_p, mem_v_p,
            delta_s, conv_s, shape_k(k_s, bs, ts), shape_v(v_s, bs, ts))
```

```python
import functools
import math

import jax
import jax.numpy as jnp
from jax import lax
from jax.experimental import pallas as pl
from jax.experimental.pallas import tpu as pltpu

F32 = jnp.float32
BF16 = jnp.bfloat16

EPS = 1e-6
CHUNK = 64
HEAD_DIM = 128
MEM_HEADS = 4
MEM_WIDTH = MEM_HEADS * HEAD_DIM
CONV_WIDTH = 4
N_BUCKETS = 32
MAX_DISTANCE = 128
NEG_INF = -1e30
LOG2E = math.log2(math.e)

LANES = 128
SUBLANES = 8
VMEM_LIMIT = 56 * 1024 * 1024

ROW_TILE = 512
FFN_ROW_TILE = 1024
FFN_VMEM_LIMIT = 60 * 1024 * 1024
ATTN_Q_TILE = 1024
ATTN_K_TILE = 512
GDN_CHUNKS = 4
GDN_HEADS = 4


def _params(*sem):
    return pltpu.CompilerParams(dimension_semantics=sem, vmem_limit_bytes=VMEM_LIMIT)


def _dot(a, b, precision=None):
    return jnp.dot(a, b, preferred_element_type=F32, precision=precision)


def _dot_nt(a, b):
    return lax.dot_general(a, b, (((1,), (1,)), ((), ())), preferred_element_type=F32)


def _dot_tn(a, b):
    return lax.dot_general(a, b, (((0,), (0,)), ((), ())), preferred_element_type=F32)


def _rms(x, gain):
    return x * lax.rsqrt(jnp.mean(x * x, axis=-1, keepdims=True) + EPS) * gain


def _silu(x):
    return x * jax.nn.sigmoid(x)


def _row_tile(rows, want):
    t = min(want, rows)
    assert rows % t == 0, (rows, t)
    return t


def _ffn_body(nf, x_ref, g0_ref, g1_ref, wg_ref, wu_ref, wd_ref, o_ref, xn_ref):
    f = pl.program_id(1)

    def partial_product():
        xn = xn_ref[...]
        gate = _dot(xn, wg_ref[...])
        up = _dot(xn, wu_ref[...])
        return _dot((_silu(gate) * up).astype(BF16), wd_ref[...])

    def finish(acc):
        o_ref[...] = x_ref[...] + 0.5 * _rms(acc, g1_ref[...])

    @pl.when(f == 0)
    def _():
        xn_ref[...] = _rms(x_ref[...], g0_ref[...]).astype(BF16)
        if nf == 1:
            finish(partial_product())
        else:
            o_ref[...] = partial_product()

    if nf > 1:
        @pl.when((f > 0) & (f < nf - 1))
        def _():
            o_ref[...] += partial_product()

        @pl.when(f == nf - 1)
        def _():
            finish(o_ref[...] + partial_product())


def _ffn(x, g0, g1, w_gu, w_down, layer, half, *, tf=512):
    rows, d = x.shape
    dff = w_down.shape[2]
    tm = _row_tile(rows, FFN_ROW_TILE)
    tf = _row_tile(dff, tf)
    nf = dff // tf
    return pl.pallas_call(
        functools.partial(_ffn_body, nf),
        grid=(rows // tm, nf),
        in_specs=[
            pl.BlockSpec((tm, d), lambda i, f: (i, 0), pipeline_mode=pl.Buffered(1)),
            pl.BlockSpec((1, d), lambda i, f: (0, 0)),
            pl.BlockSpec((1, d), lambda i, f: (0, 0)),
            pl.BlockSpec((None, None, d, tf), lambda i, f: (layer, half, 0, f)),
            pl.BlockSpec((None, None, d, tf), lambda i, f: (layer, half, 0, f + nf)),
            pl.BlockSpec((None, None, tf, d), lambda i, f: (layer, half, f, 0)),
        ],
        out_specs=pl.BlockSpec((tm, d), lambda i, f: (i, 0)),
        out_shape=jax.ShapeDtypeStruct((rows, d), F32),
        scratch_shapes=[pltpu.VMEM((tm, d), BF16)],
        compiler_params=pltpu.CompilerParams(dimension_semantics=("parallel", "arbitrary"),
                                             vmem_limit_bytes=FFN_VMEM_LIMIT),
        name="ffn",
    )(x, g0.reshape(1, d), g1.reshape(1, d), w_gu, w_gu, w_down)


def _proj_body(n_main, has_side, with_bf16, *refs):
    x_ref, g_ref = refs[0], refs[1]
    w_refs = refs[2:2 + n_main]
    pos = 2 + n_main
    ws_ref = refs[pos] if has_side else None
    pos += int(has_side)
    o_refs = refs[pos:pos + n_main]
    pos += n_main
    o16_refs = refs[pos:pos + n_main] if with_bf16 else [None] * n_main
    pos += n_main * int(with_bf16)
    os_ref = refs[pos] if has_side else None
    pos += int(has_side)
    xn_ref = refs[pos]

    @pl.when(pl.program_id(1) == 0)
    def _():
        xn_ref[...] = _rms(x_ref[...], g_ref[...]).astype(BF16)
        if has_side:
            os_ref[...] = _dot(xn_ref[...], ws_ref[...])

    xn = xn_ref[...]
    for w_ref, o_ref, o16_ref in zip(w_refs, o_refs, o16_refs):
        y = _dot(xn, w_ref[...])
        if len(o_ref.shape) == 2:
            o_ref[...] = y
        else:
            _, nh, nm, dh = o_ref.shape
            for hh in range(nh):
                for mm in range(nm):
                    c0 = (hh * nm + mm) * dh
                    o_ref[:, hh, mm, :] = y[:, c0:c0 + dh]
        if with_bf16:
            o16_ref[...] = y.astype(BF16)


def _proj(x, g, w_mains, w_side=None, *, tm, tn, with_bf16=False, split_first=None, name="proj"):
    rows, d = x.shape
    n = w_mains[0].shape[1]
    assert all(w.shape == (d, n) for w in w_mains)
    tm = _row_tile(rows, tm)
    tn = _row_tile(n, tn)
    has_side = w_side is not None
    in_specs = [pl.BlockSpec((tm, d), lambda i, j: (i, 0)), pl.BlockSpec((1, d), lambda i, j: (0, 0))]
    in_specs += [pl.BlockSpec((d, tn), lambda i, j: (0, j)) for _ in w_mains]
    out_specs = [pl.BlockSpec((tm, tn), lambda i, j: (i, j)) for _ in w_mains]
    out_shape = [jax.ShapeDtypeStruct((rows, n), F32) for _ in w_mains]
    if split_first is not None:
        nm = split_first
        group = nm * HEAD_DIM
        out_specs[0] = pl.BlockSpec((tm, tn // group, nm, HEAD_DIM), lambda i, j: (i, j, 0, 0))
        out_shape[0] = jax.ShapeDtypeStruct((rows, n // group, nm, HEAD_DIM), F32)
    if with_bf16:
        out_specs += [pl.BlockSpec((tm, tn), lambda i, j: (i, j)) for _ in w_mains]
        out_shape += [jax.ShapeDtypeStruct((rows, n), BF16) for _ in w_mains]
    args = [x, g.reshape(1, d), *w_mains]
    if has_side:
        ns = w_side.shape[1]
        in_specs.append(pl.BlockSpec((d, ns), lambda i, j: (0, 0)))
        out_specs.append(pl.BlockSpec((tm, ns), lambda i, j: (i, 0)))
        out_shape.append(jax.ShapeDtypeStruct((rows, ns), F32))
        args.append(w_side)
    return pl.pallas_call(
        functools.partial(_proj_body, len(w_mains), has_side, with_bf16),
        grid=(rows // tm, n // tn),
        in_specs=in_specs,
        out_specs=out_specs,
        out_shape=out_shape,
        scratch_shapes=[pltpu.VMEM((tm, d), BF16)],
        compiler_params=_params("parallel", "arbitrary"),
        name=name,
    )(*args)


def _gdn_body(nc, hg, n_heads, q_ref, k_ref, v_ref, z_ref, ba_ref, cq_ref, ck_ref, cv_ref, wq_ref, wk_ref, wv_ref,
              alog_ref, dtb_ref, onorm_ref, s0_ref, o_ref, sout_ref, xs_ref, state_ref):
    head0 = pl.program_id(1) * hg
    t = pl.program_id(2)
    tr = nc * CHUNK
    halo = SUBLANES

    @pl.when(t == 0)
    def _():
        xs_ref[0, 0:halo, :] = cq_ref[...]
        xs_ref[1, 0:halo, :] = ck_ref[...]
        xs_ref[2, 0:halo, :] = cv_ref[...]
        state_ref[...] = s0_ref[...]

    def conv(c, x_ref, w_ref):
        xs_ref[c, halo:halo + tr, :] = x_ref[...]
        w = w_ref[...]
        y = sum(xs_ref[c, halo - 3 + j:halo - 3 + j + tr, :] * w[j:j + 1, :] for j in range(CONV_WIDTH))
        xs_ref[c, 0:halo, :] = xs_ref[c, tr:tr + halo, :]
        return _silu(y)

    q_all = conv(0, q_ref, wq_ref)
    k_all = conv(1, k_ref, wk_ref)
    v_all = conv(2, v_ref, wv_ref)

    ba = ba_ref[...]
    g_all =-jnp.exp(alog_ref[...]) * jax.nn.softplus(ba + dtb_ref[...])

    row = lax.broadcasted_iota(jnp.int32, (CHUNK, 2 * CHUNK), 0)
    lane2 = lax.broadcasted_iota(jnp.int32, (CHUNK, 2 * CHUNK), 1)
    col = lane2 & (CHUNK - 1)
    left = lane2 < CHUNK
    incl = row >= col
    zeros_c = jnp.zeros((CHUNK, HEAD_DIM), F32)
    chunks = [slice(c * CHUNK, (c + 1) * CHUNK) for c in range(nc)]

    ctx = []
    for hd, sl in [(hd, sl) for hd in range(hg) for sl in chunks]:
        hs = slice(hd * HEAD_DIM, (hd + 1) * HEAD_DIM)
        qc, kc, vc = q_all[sl, hs], k_all[sl, hs], v_all[sl, hs]
        qc = qc * lax.rsqrt(jnp.sum(qc * qc, axis=-1, keepdims=True) + EPS) * (HEAD_DIM ** -0.5)
        kc = kc * lax.rsqrt(jnp.sum(kc * kc, axis=-1, keepdims=True) + EPS)
        bc = jax.nn.sigmoid(jnp.sum(jnp.where(lane2 == head0 + hd, ba[sl], 0.0), axis=1, keepdims=True))
        gc = jnp.sum(jnp.where(lane2 == head0 + hd + n_heads, g_all[sl], 0.0), axis=1, keepdims=True)
        g_row = jnp.sum(jnp.where(row == col, gc, 0.0), axis=0, keepdims=True)
        cum_col = jnp.sum(jnp.where(left, jnp.where(incl, g_row, 0.0), 0.0), axis=1, keepdims=True)
        cum_row = jnp.sum(jnp.where(row <= col, gc, 0.0), axis=0, keepdims=True)
        g_last = jnp.sum(gc, axis=0, keepdims=True)
        decay = jnp.where(incl, jnp.exp(jnp.where(incl, cum_col - cum_row, 0.0)), 0.0)
        kb = kc * bc
        kk_qk = _dot_nt(jnp.concatenate([kb, qc], axis=0).astype(BF16),
                        jnp.concatenate([kc, kc], axis=0).astype(BF16))
        lower = jnp.where(row > col, kk_qk[:CHUNK] * decay, 0.0)
        a_intra = jnp.where(incl, kk_qk[CHUNK:] * decay, 0.0)
        e_cum = jnp.exp(cum_col)
        ctx.append(dict(
            hd=hd, sl=sl, hs=hs,
            x=jnp.where(left, jnp.where(row == col, 1.0, 0.0), -lower),
            y=jnp.where(left, -lower, jnp.where(row == col, 1.0, 0.0)),
            rhs=jnp.concatenate([vc * bc, kb * e_cum], axis=1),
            a=a_intra.astype(BF16), qg=qc * e_cum,
            kd=(kc * jnp.exp(g_last - cum_col)).astype(BF16), gl=jnp.exp(g_last)))

    def split(a):
        hi16 = a.astype(BF16)
        hi = hi16.astype(F32)
        lo = a - hi
        return hi16, hi, lo.astype(BF16), lo

    zeros_rhs = jnp.zeros((CHUNK, 4 * CHUNK), BF16)
    for _ in range(6):
        for cx in ctx:
            xh16, xh, xl16, _ = split(cx["x"])
            yh16, yh, yl16, yl = split(cx["y"])
            lhs = jnp.concatenate([jnp.where(left, yl, xh), jnp.where(left, yh, 0.0)], axis=1).astype(BF16)
            top = jnp.concatenate([xh16, yh16], axis=1)
            rhs = jnp.concatenate([top, jnp.concatenate([xl16, yl16], axis=1), top, zeros_rhs], axis=0)
            prod = _dot(lhs, rhs)
            cx["x"] = prod[:, :2 * CHUNK] + jnp.where(left, cx["x"], 0.0)
            cx["y"] = prod[:, 2 * CHUNK:] + jnp.where(left, 0.0, cx["y"])

    for cx in ctx:
        t16 = jnp.where(left, cx["x"], 0.0).astype(BF16)
        uw = _dot(t16, jnp.concatenate([cx["rhs"], jnp.zeros_like(cx["rhs"])], axis=0).astype(BF16))
        cx["u"] = uw[:, :HEAD_DIM]
        cx["wq"] = jnp.concatenate([uw[:, HEAD_DIM:], cx["qg"]], axis=0).astype(BF16)

    states = [state_ref[hd] for hd in range(hg)]
    for cx in sorted(ctx, key=lambda cx: (cx["sl"].start, cx["hd"])):
        hd, sl, hs = cx["hd"], cx["sl"], cx["hs"]
        ws = _dot(cx["wq"], states[hd].astype(BF16))
        v_new = cx["u"] - ws[:CHUNK]
        o = ws[CHUNK:] + _dot(cx["a"], jnp.concatenate([v_new, zeros_c], axis=0).astype(BF16))
        states[hd] = states[hd] * cx["gl"] + _dot_tn(cx["kd"], v_new.astype(BF16))
        o_ref[sl, hs] = _rms(o, onorm_ref[...]) * _silu(z_ref[sl, hs])

    for hd in range(hg):
        state_ref[hd] = states[hd]
        sout_ref[hd] = states[hd]


def _gdn(proj, ba, conv_buf, conv_w, a_log, dt_bias, onorm, s0, *, nc, hg):
    bsz, tlen, _ = proj.shape
    n_heads = s0.shape[1]
    tr = nc * CHUNK
    assert tlen % tr == 0 and n_heads % hg == 0
    nt = tlen // tr
    ngroups = n_heads // hg
    width = hg * HEAD_DIM
    pad = lambda a: jnp.zeros((1, LANES), F32).at[0, n_heads:2 * n_heads].set(a)

    def col(c):
        return pl.BlockSpec((None, tr, width), lambda b, h, t: (b, t, c * ngroups + h))

    def buf(c):
        return pl.BlockSpec((None, SUBLANES, width), lambda b, h, t: (b, 0, c * ngroups + h))

    def tap(c):
        return pl.BlockSpec((CONV_WIDTH, width), lambda b, h, t: (0, c * ngroups + h))

    vec = pl.BlockSpec((1, LANES), lambda b, h, t: (0, 0))
    state_spec = pl.BlockSpec((None, hg, HEAD_DIM, HEAD_DIM), lambda b, h, t: (b, h, 0, 0))
    return pl.pallas_call(
        functools.partial(_gdn_body, nc, hg, n_heads),
        grid=(bsz, ngroups, nt),
        in_specs=[col(0), col(1), col(2), col(3),
                  pl.BlockSpec((None, tr, LANES), lambda b, h, t: (b, t, 0)),
                  buf(0), buf(1), buf(2), tap(0), tap(1), tap(2), vec, vec, vec, state_spec],
        out_specs=[pl.BlockSpec((None, tr, width), lambda b, h, t: (b, t, h)), state_spec],
        out_shape=[jax.ShapeDtypeStruct((bsz, tlen, n_heads * HEAD_DIM), F32),
                   jax.ShapeDtypeStruct(s0.shape, F32)],
        scratch_shapes=[pltpu.VMEM((3, tr + SUBLANES, width), F32), pltpu.VMEM((hg, HEAD_DIM, HEAD_DIM), F32)],
        compiler_params=_params("parallel", "parallel", "arbitrary"),
        name="gdn",
    )(proj, proj, proj, proj, ba, conv_buf, conv_buf, conv_buf, conv_w, conv_w, conv_w,
      pad(a_log), pad(dt_bias), onorm.reshape(1, HEAD_DIM), s0)


def _diff_body(causal, ratio, lam0, it_ref, jt_ref, far_ref, q_ref, k_ref, v_ref, bias_ref, lq_ref, sub_ref,
               o_ref, qs_ref, s_ref, m_ref, l_ref, acc_ref):
    step = pl.program_id(2)
    i = it_ref[step]
    j = jt_ref[step]
    last = (i + 1) * ratio - 1 if causal else pl.num_programs(2) - 2
    first_near = i * ratio - 1 if causal else 0
    tk = k_ref.shape[0]
    dk = HEAD_DIM
    tq = q_ref.shape[0]
    rb = 256 if tq % 256 == 0 else tq
    units = [(mp, slice(r * rb, (r + 1) * rb)) for r in range(tq // rb) for mp in range(2)]

    @pl.when(j == 0)
    def _():
        q = q_ref[...] * (dk ** -0.5 * LOG2E)
        qs_ref[0] = q[:, :dk].astype(BF16)
        qs_ref[1] = q[:, dk:].astype(BF16)
        m_ref[...] = jnp.full(m_ref.shape, NEG_INF, F32)
        l_ref[...] = jnp.zeros(l_ref.shape, F32)
        acc_ref[...] = jnp.zeros(acc_ref.shape, F32)

    def scores(slot):
        for mp, rows in units:
            s_ref[slot, mp, rows, :] = _dot_nt(qs_ref[mp, rows, :], k_ref[:, mp * dk:(mp + 1) * dk])

    def update(slot, bias, shift):
        v = v_ref[...]
        probs = []
        for mp, rows in units:
            s = s_ref[slot, mp, rows, :]
            if bias is not None:
                s = s + bias[rows]
            m_prev = m_ref[mp, rows, :]
            m_next = jnp.maximum(m_prev, jnp.max(s, axis=1, keepdims=True) + shift)
            p = jnp.exp2(s - jnp.concatenate([m_next - shift] * (tk // LANES), axis=1))
            alpha = jnp.exp2(m_prev - m_next)
            l_ref[mp, rows, :] = alpha * l_ref[mp, rows, :] + jnp.sum(p, axis=1, keepdims=True)
            m_ref[mp, rows, :] = m_next
            probs.append((p.astype(BF16), alpha))
        for (mp, rows), (p16, alpha) in zip(units, probs):
            acc_ref[mp, rows, :] = (acc_ref[mp, rows, :] * jnp.concatenate([alpha] * (2 * dk // LANES), axis=1)
                                    + _dot(p16, v))

    @pl.when(j == 0)
    def _():
        scores(0)

    for cur in range(2):
        prev = 1 - cur
        par = (j % 2) == cur
        if causal:
            @pl.when(par & (j >= 1) & (j - 1 < first_near))
            def _():
                scores(cur)
                update(prev, None, far_ref[pl.program_id(1)])

        @pl.when(par & (j >= 1) & (j - 1 >= first_near) & (j <= last))
        def _():
            scores(cur)
            update(prev, bias_ref[...], 0.0)

        @pl.when(par & (j == last + 1))
        def _():
            update(prev, bias_ref[...], 0.0)

    @pl.when(j == last + 1)
    def _():
        lq = lq_ref[...]
        lam = (jnp.exp(jnp.sum(lq[0:1] * lq[1:2], axis=1, keepdims=True))
               - jnp.exp(jnp.sum(lq[2:3] * lq[3:4], axis=1, keepdims=True)) + lam0)
        reps = 2 * dk // LANES
        o0 = acc_ref[0] / jnp.concatenate([l_ref[0]] * reps, axis=1)
        o1 = acc_ref[1] / jnp.concatenate([l_ref[1]] * reps, axis=1)
        o_ref[...] = _rms(o0 - lam * o1, sub_ref[...]) * (1.0 - lam0)


def _diff_attention(q, k, v, bias, far, lam_qk, subln, lam0, *, n_heads, tq, tk, causal):
    bsz, tq_len, _ = q.shape
    tk_len = k.shape[1]
    nq, nk = tq_len // tq, tk_len // tk
    dv = 2 * HEAD_DIM
    assert tq % tk == 0 if causal else nq == 1
    ratio = tq // tk if causal else 1

    def last_tile(i):
        return (i + 1) * ratio - 1 if causal else nk - 1

    pairs = [(i, j) for i in range(nq) for j in range(last_tile(i) + 2)]
    i_tab = jnp.asarray([p[0] for p in pairs], jnp.int32)
    j_tab = jnp.asarray([p[1] for p in pairs], jnp.int32)

    def q_map(b, h, s, it, jt):
        return (b, it[s], h)

    def k_map(b, h, s, it, jt):
        return (b, jnp.minimum(jt[s], last_tile(it[s])), h)

    def v_map(b, h, s, it, jt):
        return (b, jnp.clip(jt[s] - 1, 0, last_tile(it[s])), h)

    def bias_map(b, h, s, it, jt):
        if causal:
            return (h, jnp.clip(jt[s] - it[s] * ratio, 0, ratio), 0, 0)
        return (h, jnp.clip(jt[s] - 1, 0, nk - 1), 0, 0)

    const2 = lambda b, h, s, it, jt: (0, 0)
    return pl.pallas_call(
        functools.partial(_diff_body, causal, ratio, lam0),
        grid_spec=pltpu.PrefetchScalarGridSpec(
            num_scalar_prefetch=2,
            grid=(bsz, n_heads, len(pairs)),
            in_specs=[
                pl.BlockSpec(memory_space=pltpu.SMEM),
                pl.BlockSpec((None, tq, dv), q_map),
                pl.BlockSpec((None, tk, dv), k_map),
                pl.BlockSpec((None, tk, dv), v_map),
                pl.BlockSpec((None, None, tq, tk), bias_map),
                pl.BlockSpec((4, HEAD_DIM), const2),
                pl.BlockSpec((1, dv), const2),
            ],
            out_specs=pl.BlockSpec((None, tq, dv), q_map),
            scratch_shapes=[pltpu.VMEM((2, tq, HEAD_DIM), BF16), pltpu.VMEM((2, 2, tq, tk), F32),
                            pltpu.VMEM((2, tq, LANES), F32), pltpu.VMEM((2, tq, LANES), F32),
                            pltpu.VMEM((2, tq, dv), F32)],
        ),
        out_shape=jax.ShapeDtypeStruct((bsz, tq_len, n_heads * dv), F32),
        compiler_params=_params("parallel", "parallel", "arbitrary"),
        name="diff_attn",
    )(i_tab, j_tab, far, q, k, v, bias, lam_qk, subln.reshape(1, dv))


def _rel_bucket(rel):
    nb = N_BUCKETS // 2
    max_exact = nb // 2
    n = jnp.abs(rel)
    large = max_exact + (jnp.log(jnp.maximum(n, 1).astype(F32) / max_exact)
                         / math.log(MAX_DISTANCE / max_exact) * (nb - max_exact)).astype(jnp.int32)
    large = jnp.minimum(large, nb - 1)
    return jnp.where(rel > 0, nb, 0) + jnp.where(n < max_exact, n, large)


def _bias_window(rel_table, q_pos, k_pos):
    bucket = _rel_bucket(k_pos[None, :] - q_pos[:, None])[None]
    bias = sum(jnp.where(bucket == b, rel_table[b][:, None, None], 0.0) for b in range(N_BUCKETS))
    visible = (k_pos[None, :] // CHUNK) <= (q_pos[:, None] // CHUNK)
    return jnp.where(visible[None], bias, NEG_INF).astype(F32)


def _mix_out_body(o_ref, qm_ref, mk_ref, mv_ref, w_ref, h_ref, g_ref, out_ref, mix_ref):
    n_o = o_ref.shape[1]
    mix_ref[:, 0:n_o] = o_ref[...].astype(BF16)
    for hd in range(MEM_HEADS):
        sl = slice(hd * HEAD_DIM, (hd + 1) * HEAD_DIM)
        s = _dot_nt(qm_ref[:, sl].astype(BF16), mk_ref[:, sl]) * (HEAD_DIM ** -0.5)
        e = jnp.exp(s - jnp.max(s, axis=1, keepdims=True))
        p = e / jnp.sum(e, axis=1, keepdims=True)
        mix_ref[:, n_o + hd * HEAD_DIM:n_o + (hd + 1) * HEAD_DIM] = _dot(p.astype(BF16), mv_ref[:, sl]).astype(BF16)
    y = _dot(mix_ref[...], w_ref[...])
    out_ref[...] = h_ref[...] + _rms(y, g_ref[...])


def _mix_out(o, qm_src, qm_block, mem_k, mem_v, w_out, h, g):
    bsz, tlen, n_o = o.shape
    d = h.shape[2]
    mlen = mem_k.shape[1]
    tm = _row_tile(tlen, ROW_TILE)
    return pl.pallas_call(
        _mix_out_body,
        grid=(bsz, tlen // tm),
        in_specs=[
            pl.BlockSpec((None, tm, n_o), lambda b, i: (b, i, 0)),
            pl.BlockSpec((None, tm, MEM_WIDTH), lambda b, i: (b, i, qm_block)),
            pl.BlockSpec((None, mlen, MEM_WIDTH), lambda b, i: (b, 0, 0)),
            pl.BlockSpec((None, mlen, MEM_WIDTH), lambda b, i: (b, 0, 0)),
            pl.BlockSpec((d, d), lambda b, i: (0, 0)),
            pl.BlockSpec((None, tm, d), lambda b, i: (b, i, 0)),
            pl.BlockSpec((1, d), lambda b, i: (0, 0)),
        ],
        out_specs=pl.BlockSpec((None, tm, d), lambda b, i: (b, i, 0)),
        out_shape=jax.ShapeDtypeStruct(h.shape, F32),
        scratch_shapes=[pltpu.VMEM((tm, d), BF16)],
        compiler_params=_params("parallel", "parallel"),
        name="mix_out",
    )(o, qm_src, mem_k, mem_v, w_out, h, g.reshape(1, d))


def _lambda_init(layer_idx):
    return 0.8 - 0.6 * math.exp(-0.3 * layer_idx)


def _prep_weights(norm_gains, ffn_gate_up, ffn_down, w_out, w_mem_kv, w_in_a, conv_w_a, w_kv, w_in_b):
    d = w_out.shape[1]
    n_a = w_in_a.shape[0]
    conv_dim = conv_w_a.shape[2]
    mix = d - MEM_WIDTH
    n_heads = mix // HEAD_DIM
    qkvz = conv_dim + mix
    w = {
        "gu": ffn_gate_up.astype(BF16), "down": ffn_down.astype(BF16), "out": w_out.astype(BF16),
        "mem_k": w_mem_kv[:, :, :MEM_WIDTH].astype(BF16), "mem_v": w_mem_kv[:, :, MEM_WIDTH:].astype(BF16),
        "in_a": jnp.concatenate([w_in_a[:, :, :qkvz], w_in_a[:, :, qkvz + 2 * n_heads:]], axis=2).astype(BF16),
        "in_a_ba": jnp.pad(w_in_a[:, :, qkvz:qkvz + 2 * n_heads],
                           ((0, 0), (0, 0), (0, LANES - 2 * n_heads))).astype(BF16),
        "kv_k": w_kv[:, :mix].astype(BF16), "kv_v": w_kv[:, mix:].astype(BF16),
        "in_b": w_in_b.astype(BF16),
    }
    return w, n_a, n_heads, mix


def _run_group(x, mem_k, mem_v, conv_states, delta_states, past_k, past_v, q_pos0, W, P, *, blocked):
    w, n_a, n_heads, mix = W
    bsz, tlen, d = x.shape
    rows = bsz * tlen
    depth = P["norm_gains"].shape[0]
    diff_heads = mix // (2 * HEAD_DIM)
    h = x.reshape(rows, d)
    new_conv, new_delta = [], []
    k_new = v_new = None
    for l in range(depth):
        ng = P["norm_gains"][l]
        if l == n_a:
            k_new, v_new, k16, v16 = _proj(h, P["kv_norm"], [w["kv_k"], w["kv_v"]], tm=512, tn=768,
                                           with_bf16=True, split_first=2, name="proj_kv")
        h = _ffn(h, ng[0], ng[1], w["gu"], w["down"], l, 0)
        if l < n_a:
            proj, ba = _proj(h, ng[2], [w["in_a"][l]], w["in_a_ba"][l], tm=1024, tn=1664, name="proj_a")
            n_proj = proj.shape[1]
            proj = proj.reshape(bsz, tlen, n_proj)
            conv_dim = 3 * mix
            new_conv.append(proj[:, tlen - (CONV_WIDTH - 1):, :conv_dim])
            conv_buf = jnp.pad(conv_states[l], ((0, 0), (SUBLANES - (CONV_WIDTH - 1), 0), (0, 0)))
            o, s_out = _gdn(proj, ba.reshape(bsz, tlen, LANES), conv_buf, P["conv_w_a"][l], P["a_log"][l],
                            P["dt_bias"][l], P["onorm_a"][l], delta_states[l], nc=GDN_CHUNKS if tlen % (GDN_CHUNKS * CHUNK) == 0 else 1, hg=GDN_HEADS)
            new_delta.append(s_out)
            qm_src, qm_block = proj, (conv_dim + mix) // MEM_WIDTH
        else:
            jb = l - n_a
            (proj,) = _proj(h, ng[2], [w["in_b"][jb]], tm=512, tn=2048, name="proj_b")
            proj = proj.reshape(bsz, tlen, d)
            lam0 = _lambda_init(l)
            k3 = k16.reshape(bsz, tlen, mix)
            v3 = v16.reshape(bsz, tlen, mix)
            if blocked:
                tq = _row_tile(tlen, ATTN_Q_TILE)
                tk = _row_tile(tq, ATTN_K_TILE)
                band = tq // tk + 1
                near = _bias_window(P["rel_bias"], tq + jnp.arange(tq), tq - tk + jnp.arange(band * tk))
                bias = jnp.stack([near[:, :, n * tk:(n + 1) * tk] for n in range(band)], axis=1) * LOG2E
                assert tk >= MAX_DISTANCE and tq % CHUNK == 0
                far = P["rel_bias"][_rel_bucket(jnp.int32(-2 * MAX_DISTANCE))] * LOG2E
                o = _diff_attention(proj, k3, v3, bias, far, P["lambda_qk"][jb], P["subln_b"][jb], lam0,
                                    n_heads=diff_heads, tq=tq, tk=tk, causal=True)
            else:
                past = past_k.shape[1]
                total = past + tlen
                padded = -(-total // LANES) * LANES
                grow = lambda c, n: jnp.pad(jnp.concatenate([c.reshape(bsz, past, mix).astype(BF16), n], axis=1),
                                            ((0, 0), (0, padded - total), (0, 0)))
                k_pos = jnp.arange(padded)
                bias = _bias_window(P["rel_bias"], q_pos0 + jnp.arange(tlen), k_pos)
                bias = jnp.where(k_pos[None, None, :] < total, bias, NEG_INF)[:, None] * LOG2E
                o = _diff_attention(proj, grow(past_k, k3), grow(past_v, v3), bias, jnp.zeros((diff_heads,), F32),
                                    P["lambda_qk"][jb], P["subln_b"][jb], lam0, n_heads=diff_heads, tq=tlen,
                                    tk=padded, causal=False)
            qm_src, qm_block = proj, mix // MEM_WIDTH
        h = _mix_out(o, qm_src, qm_block, mem_k[l], mem_v[l], w["out"][l], h.reshape(bsz, tlen, d), ng[3])
        h = _ffn(h.reshape(rows, d), ng[4], ng[5], w["gu"], w["down"], l, 1)
    return h.reshape(bsz, tlen, d), jnp.stack(new_conv), jnp.stack(new_delta), k_new, v_new


def kernel(x_prompt, x_sample, mem_prompt, cache_k, cache_v, cache_mem_k, cache_mem_v, state_delta, state_conv,
           norm_gains, ffn_gate_up, ffn_down, w_out, mem_norm, w_mem_kv, w_in_a, conv_w_a, a_log, dt_bias,
           onorm_a, kv_norm, w_kv, w_in_b, lambda_qk, subln_b, rel_bias):
    W = _prep_weights(norm_gains, ffn_gate_up, ffn_down, w_out, w_mem_kv, w_in_a, conv_w_a, w_kv, w_in_b)
    w, n_a, n_heads, mix = W
    P = {"norm_gains": norm_gains, "kv_norm": kv_norm, "conv_w_a": conv_w_a, "a_log": a_log, "dt_bias": dt_bias,
         "onorm_a": onorm_a, "lambda_qk": lambda_qk, "subln_b": subln_b, "rel_bias": rel_bias}
    depth = norm_gains.shape[0]
    bp, tp, d = x_prompt.shape
    bs, ts, _ = x_sample.shape
    mlen = mem_prompt.shape[1]
    diff_heads = mix // (2 * HEAD_DIM)

    mem_rows = mem_prompt.reshape(bp * mlen, d)
    mem_kv = [_proj(mem_rows, mem_norm[l], [w["mem_k"][l], w["mem_v"][l]], tm=256, tn=512, name="proj_mem")
              for l in range(depth)]
    mem_k_p = jnp.stack([kv[0] for kv in mem_kv]).reshape(depth, bp, mlen, MEM_HEADS, HEAD_DIM)
    mem_v_p = jnp.stack([kv[1] for kv in mem_kv]).reshape(depth, bp, mlen, MEM_HEADS, HEAD_DIM)
    conv0 = jnp.zeros((n_a, bp, CONV_WIDTH - 1, 3 * mix), F32)
    delta0 = jnp.zeros((n_a, bp, n_heads, HEAD_DIM, HEAD_DIM), F32)
    flat = lambda m: m.reshape(m.shape[0], m.shape[1], m.shape[2], MEM_WIDTH).astype(BF16)
    y_p, conv_p, delta_p, k_p, v_p = _run_group(
        x_prompt, flat(mem_k_p), flat(mem_v_p), conv0, delta0, None, None, 0, W, P, blocked=True)

    y_s, conv_s, delta_s, k_s, v_s = _run_group(
        x_sample, flat(cache_mem_k), flat(cache_mem_v), state_conv, state_delta, cache_k, cache_v,
        cache_k.shape[1], W, P, blocked=False)

    shape_k = lambda a, b, t: a.reshape(b, t, diff_heads, 2, HEAD_DIM)
    shape_v = lambda a, b, t: a.reshape(b, t, diff_heads, 2 * HEAD_DIM)
    return (y_p, y_s, delta_p, conv_p, shape_k(k_p, bp, tp), shape_v(v_p, bp, tp), mem_k_p, mem_v_p,
            delta_s, conv_s, shape_k(k_s, bs, ts), shape_v(v_s, bs, ts))
```

```python
import functools
import math

import jax
import jax.numpy as jnp
from jax import lax
from jax.experimental import pallas as pl
from jax.experimental.pallas import tpu as pltpu

F32 = jnp.float32
BF16 = jnp.bfloat16

EPS = 1e-6
CHUNK = 64
HEAD_DIM = 128
MEM_HEADS = 4
MEM_WIDTH = MEM_HEADS * HEAD_DIM
CONV_WIDTH = 4
N_BUCKETS = 32
MAX_DISTANCE = 128
NEG_INF = -1e30
LOG2E = math.log2(math.e)

LANES = 128
SUBLANES = 8
VMEM_LIMIT = 56 * 1024 * 1024

ROW_TILE = 512
FFN_ROW_TILE = 1024
FFN_VMEM_LIMIT = 60 * 1024 * 1024
ATTN_Q_TILE = 1024
ATTN_K_TILE = 1024
GDN_CHUNKS = 4
GDN_HEADS = 4


def _params(*sem):
    return pltpu.CompilerParams(dimension_semantics=sem, vmem_limit_bytes=VMEM_LIMIT)


def _dot(a, b, precision=None):
    return jnp.dot(a, b, preferred_element_type=F32, precision=precision)


def _dot_nt(a, b):
    return lax.dot_general(a, b, (((1,), (1,)), ((), ())), preferred_element_type=F32)


def _dot_tn(a, b):
    return lax.dot_general(a, b, (((0,), (0,)), ((), ())), preferred_element_type=F32)


def _rms(x, gain):
    return x * lax.rsqrt(jnp.mean(x * x, axis=-1, keepdims=True) + EPS) * gain


def _silu(x):
    return x * jax.nn.sigmoid(x)


def _row_tile(rows, want):
    t = min(want, rows)
    assert rows % t == 0, (rows, t)
    return t


def _ffn_body(nf, x_ref, g0_ref, g1_ref, wg_ref, wu_ref, wd_ref, o_ref, xn_ref):
    f = pl.program_id(1)

    def partial_product():
        xn = xn_ref[...]
        gate = _dot(xn, wg_ref[...])
        up = _dot(xn, wu_ref[...])
        return _dot((_silu(gate) * up).astype(BF16), wd_ref[...])

    def finish(acc):
        o_ref[...] = x_ref[...] + 0.5 * _rms(acc, g1_ref[...])

    @pl.when(f == 0)
    def _():
        xn_ref[...] = _rms(x_ref[...], g0_ref[...]).astype(BF16)
        if nf == 1:
            finish(partial_product())
        else:
            o_ref[...] = partial_product()

    if nf > 1:
        @pl.when((f > 0) & (f < nf - 1))
        def _():
            o_ref[...] += partial_product()

        @pl.when(f == nf - 1)
        def _():
            finish(o_ref[...] + partial_product())


def _ffn(x, g0, g1, w_gu, w_down, layer, half, *, tf=512):
    rows, d = x.shape
    dff = w_down.shape[2]
    tm = _row_tile(rows, FFN_ROW_TILE)
    tf = _row_tile(dff, tf)
    nf = dff // tf
    return pl.pallas_call(
        functools.partial(_ffn_body, nf),
        grid=(rows // tm, nf),
        in_specs=[
            pl.BlockSpec((tm, d), lambda i, f: (i, 0), pipeline_mode=pl.Buffered(1)),
            pl.BlockSpec((1, d), lambda i, f: (0, 0)),
            pl.BlockSpec((1, d), lambda i, f: (0, 0)),
            pl.BlockSpec((None, None, d, tf), lambda i, f: (layer, half, 0, f)),
            pl.BlockSpec((None, None, d, tf), lambda i, f: (layer, half, 0, f + nf)),
            pl.BlockSpec((None, None, tf, d), lambda i, f: (layer, half, f, 0)),
        ],
        out_specs=pl.BlockSpec((tm, d), lambda i, f: (i, 0)),
        out_shape=jax.ShapeDtypeStruct((rows, d), F32),
        scratch_shapes=[pltpu.VMEM((tm, d), BF16)],
        compiler_params=pltpu.CompilerParams(dimension_semantics=("parallel", "arbitrary"),
                                             vmem_limit_bytes=FFN_VMEM_LIMIT),
        name="ffn",
    )(x, g0.reshape(1, d), g1.reshape(1, d), w_gu, w_gu, w_down)


def _proj_body(n_main, has_side, with_bf16, *refs):
    x_ref, g_ref = refs[0], refs[1]
    w_refs = refs[2:2 + n_main]
    pos = 2 + n_main
    ws_ref = refs[pos] if has_side else None
    pos += int(has_side)
    o_refs = refs[pos:pos + n_main]
    pos += n_main
    o16_refs = refs[pos:pos + n_main] if with_bf16 else [None] * n_main
    pos += n_main * int(with_bf16)
    os_ref = refs[pos] if has_side else None
    pos += int(has_side)
    xn_ref = refs[pos]

    @pl.when(pl.program_id(1) == 0)
    def _():
        xn_ref[...] = _rms(x_ref[...], g_ref[...]).astype(BF16)
        if has_side:
            os_ref[...] = _dot(xn_ref[...], ws_ref[...])

    xn = xn_ref[...]
    for w_ref, o_ref, o16_ref in zip(w_refs, o_refs, o16_refs):
        y = _dot(xn, w_ref[...])
        if len(o_ref.shape) == 2:
            o_ref[...] = y
        else:
            _, nh, nm, dh = o_ref.shape
            for hh in range(nh):
                for mm in range(nm):
                    c0 = (hh * nm + mm) * dh
                    o_ref[:, hh, mm, :] = y[:, c0:c0 + dh]
        if with_bf16:
            o16_ref[...] = y.astype(BF16)


def _proj(x, g, w_mains, w_side=None, *, tm, tn, with_bf16=False, split_first=None, name="proj"):
    rows, d = x.shape
    n = w_mains[0].shape[1]
    assert all(w.shape == (d, n) for w in w_mains)
    tm = _row_tile(rows, tm)
    tn = _row_tile(n, tn)
    has_side = w_side is not None
    in_specs = [pl.BlockSpec((tm, d), lambda i, j: (i, 0)), pl.BlockSpec((1, d), lambda i, j: (0, 0))]
    in_specs += [pl.BlockSpec((d, tn), lambda i, j: (0, j)) for _ in w_mains]
    out_specs = [pl.BlockSpec((tm, tn), lambda i, j: (i, j)) for _ in w_mains]
    out_shape = [jax.ShapeDtypeStruct((rows, n), F32) for _ in w_mains]
    if split_first is not None:
        nm = split_first
        group = nm * HEAD_DIM
        out_specs[0] = pl.BlockSpec((tm, tn // group, nm, HEAD_DIM), lambda i, j: (i, j, 0, 0))
        out_shape[0] = jax.ShapeDtypeStruct((rows, n // group, nm, HEAD_DIM), F32)
    if with_bf16:
        out_specs += [pl.BlockSpec((tm, tn), lambda i, j: (i, j)) for _ in w_mains]
        out_shape += [jax.ShapeDtypeStruct((rows, n), BF16) for _ in w_mains]
    args = [x, g.reshape(1, d), *w_mains]
    if has_side:
        ns = w_side.shape[1]
        in_specs.append(pl.BlockSpec((d, ns), lambda i, j: (0, 0)))
        out_specs.append(pl.BlockSpec((tm, ns), lambda i, j: (i, 0)))
        out_shape.append(jax.ShapeDtypeStruct((rows, ns), F32))
        args.append(w_side)
    return pl.pallas_call(
        functools.partial(_proj_body, len(w_mains), has_side, with_bf16),
        grid=(rows // tm, n // tn),
        in_specs=in_specs,
        out_specs=out_specs,
        out_shape=out_shape,
        scratch_shapes=[pltpu.VMEM((tm, d), BF16)],
        compiler_params=_params("parallel", "arbitrary"),
        name=name,
    )(*args)


def _gdn_body(nc, hg, n_heads, q_ref, k_ref, v_ref, z_ref, ba_ref, cq_ref, ck_ref, cv_ref, wq_ref, wk_ref, wv_ref,
              alog_ref, dtb_ref, onorm_ref, s0_ref, o_ref, sout_ref, xs_ref, state_ref):
    head0 = pl.program_id(1) * hg
    t = pl.program_id(2)
    tr = nc * CHUNK
    halo = SUBLANES

    @pl.when(t == 0)
    def _():
        xs_ref[0, 0:halo, :] = cq_ref[...]
        xs_ref[1, 0:halo, :] = ck_ref[...]
        xs_ref[2, 0:halo, :] = cv_ref[...]
        state_ref[...] = s0_ref[...]

    def conv(c, x_ref, w_ref):
        xs_ref[c, halo:halo + tr, :] = x_ref[...]
        w = w_ref[...]
        y = sum(xs_ref[c, halo - 3 + j:halo - 3 + j + tr, :] * w[j:j + 1, :] for j in range(CONV_WIDTH))
        xs_ref[c, 0:halo, :] = xs_ref[c, tr:tr + halo, :]
        return _silu(y)

    q_all = conv(0, q_ref, wq_ref)
    k_all = conv(1, k_ref, wk_ref)
    v_all = conv(2, v_ref, wv_ref)

    ba = ba_ref[...]
    g_all =-jnp.exp(alog_ref[...]) * jax.nn.softplus(ba + dtb_ref[...])

    row = lax.broadcasted_iota(jnp.int32, (CHUNK, 2 * CHUNK), 0)
    lane2 = lax.broadcasted_iota(jnp.int32, (CHUNK, 2 * CHUNK), 1)
    col = lane2 & (CHUNK - 1)
    left = lane2 < CHUNK
    incl = row >= col
    zeros_c = jnp.zeros((CHUNK, HEAD_DIM), F32)
    chunks = [slice(c * CHUNK, (c + 1) * CHUNK) for c in range(nc)]

    ctx = []
    for hd, sl in [(hd, sl) for hd in range(hg) for sl in chunks]:
        hs = slice(hd * HEAD_DIM, (hd + 1) * HEAD_DIM)
        qc, kc, vc = q_all[sl, hs], k_all[sl, hs], v_all[sl, hs]
        qc = qc * lax.rsqrt(jnp.sum(qc * qc, axis=-1, keepdims=True) + EPS) * (HEAD_DIM ** -0.5)
        kc = kc * lax.rsqrt(jnp.sum(kc * kc, axis=-1, keepdims=True) + EPS)
        bc = jax.nn.sigmoid(jnp.sum(jnp.where(lane2 == head0 + hd, ba[sl], 0.0), axis=1, keepdims=True))
        gc = jnp.sum(jnp.where(lane2 == head0 + hd + n_heads, g_all[sl], 0.0), axis=1, keepdims=True)
        g_row = jnp.sum(jnp.where(row == col, gc, 0.0), axis=0, keepdims=True)
        cum_col = jnp.sum(jnp.where(left, jnp.where(incl, g_row, 0.0), 0.0), axis=1, keepdims=True)
        cum_row = jnp.sum(jnp.where(row <= col, gc, 0.0), axis=0, keepdims=True)
        g_last = jnp.sum(gc, axis=0, keepdims=True)
        decay = jnp.where(incl, jnp.exp(jnp.where(incl, cum_col - cum_row, 0.0)), 0.0)
        kb = kc * bc
        kk_qk = _dot_nt(jnp.concatenate([kb, qc], axis=0).astype(BF16),
                        jnp.concatenate([kc, kc], axis=0).astype(BF16))
        lower = jnp.where(row > col, kk_qk[:CHUNK] * decay, 0.0)
        a_intra = jnp.where(incl, kk_qk[CHUNK:] * decay, 0.0)
        e_cum = jnp.exp(cum_col)
        ctx.append(dict(
            hd=hd, sl=sl, hs=hs,
            x=jnp.where(left, jnp.where(row == col, 1.0, 0.0), -lower),
            y=jnp.where(left, -lower, jnp.where(row == col, 1.0, 0.0)),
            rhs=jnp.concatenate([vc * bc, kb * e_cum], axis=1),
            a=a_intra.astype(BF16), qg=qc * e_cum,
            kd=(kc * jnp.exp(g_last - cum_col)).astype(BF16), gl=jnp.exp(g_last)))

    def split(a):
        hi16 = a.astype(BF16)
        hi = hi16.astype(F32)
        lo = a - hi
        return hi16, hi, lo.astype(BF16), lo

    zeros_rhs = jnp.zeros((CHUNK, 4 * CHUNK), BF16)
    for _ in range(6):
        for cx in ctx:
            xh16, xh, xl16, _ = split(cx["x"])
            yh16, yh, yl16, yl = split(cx["y"])
            lhs = jnp.concatenate([jnp.where(left, yl, xh), jnp.where(left, yh, 0.0)], axis=1).astype(BF16)
            top = jnp.concatenate([xh16, yh16], axis=1)
            rhs = jnp.concatenate([top, jnp.concatenate([xl16, yl16], axis=1), top, zeros_rhs], axis=0)
            prod = _dot(lhs, rhs)
            cx["x"] = prod[:, :2 * CHUNK] + jnp.where(left, cx["x"], 0.0)
            cx["y"] = prod[:, 2 * CHUNK:] + jnp.where(left, 0.0, cx["y"])

    for cx in ctx:
        t16 = jnp.where(left, cx["x"], 0.0).astype(BF16)
        uw = _dot(t16, jnp.concatenate([cx["rhs"], jnp.zeros_like(cx["rhs"])], axis=0).astype(BF16))
        cx["u"] = uw[:, :HEAD_DIM]
        cx["wq"] = jnp.concatenate([uw[:, HEAD_DIM:], cx["qg"]], axis=0).astype(BF16)

    states = [state_ref[hd] for hd in range(hg)]
    for cx in sorted(ctx, key=lambda cx: (cx["sl"].start, cx["hd"])):
        hd, sl, hs = cx["hd"], cx["sl"], cx["hs"]
        ws = _dot(cx["wq"], states[hd].astype(BF16))
        v_new = cx["u"] - ws[:CHUNK]
        o = ws[CHUNK:] + _dot(cx["a"], jnp.concatenate([v_new, zeros_c], axis=0).astype(BF16))
        states[hd] = states[hd] * cx["gl"] + _dot_tn(cx["kd"], v_new.astype(BF16))
        o_ref[sl, hs] = _rms(o, onorm_ref[...]) * _silu(z_ref[sl, hs])

    for hd in range(hg):
        state_ref[hd] = states[hd]
        sout_ref[hd] = states[hd]


def _gdn(proj, ba, conv_buf, conv_w, a_log, dt_bias, onorm, s0, *, nc, hg):
    bsz, tlen, _ = proj.shape
    n_heads = s0.shape[1]
    tr = nc * CHUNK
    assert tlen % tr == 0 and n_heads % hg == 0
    nt = tlen // tr
    ngroups = n_heads // hg
    width = hg * HEAD_DIM
    pad = lambda a: jnp.zeros((1, LANES), F32).at[0, n_heads:2 * n_heads].set(a)

    def col(c):
        return pl.BlockSpec((None, tr, width), lambda b, h, t: (b, t, c * ngroups + h))

    def buf(c):
        return pl.BlockSpec((None, SUBLANES, width), lambda b, h, t: (b, 0, c * ngroups + h))

    def tap(c):
        return pl.BlockSpec((CONV_WIDTH, width), lambda b, h, t: (0, c * ngroups + h))

    vec = pl.BlockSpec((1, LANES), lambda b, h, t: (0, 0))
    state_spec = pl.BlockSpec((None, hg, HEAD_DIM, HEAD_DIM), lambda b, h, t: (b, h, 0, 0))
    return pl.pallas_call(
        functools.partial(_gdn_body, nc, hg, n_heads),
        grid=(bsz, ngroups, nt),
        in_specs=[col(0), col(1), col(2), col(3),
                  pl.BlockSpec((None, tr, LANES), lambda b, h, t: (b, t, 0)),
                  buf(0), buf(1), buf(2), tap(0), tap(1), tap(2), vec, vec, vec, state_spec],
        out_specs=[pl.BlockSpec((None, tr, width), lambda b, h, t: (b, t, h)), state_spec],
        out_shape=[jax.ShapeDtypeStruct((bsz, tlen, n_heads * HEAD_DIM), F32),
                   jax.ShapeDtypeStruct(s0.shape, F32)],
        scratch_shapes=[pltpu.VMEM((3, tr + SUBLANES, width), F32), pltpu.VMEM((hg, HEAD_DIM, HEAD_DIM), F32)],
        compiler_params=_params("parallel", "parallel", "arbitrary"),
        name="gdn",
    )(proj, proj, proj, proj, ba, conv_buf, conv_buf, conv_buf, conv_w, conv_w, conv_w,
      pad(a_log), pad(dt_bias), onorm.reshape(1, HEAD_DIM), s0)


def _diff_body(causal, ratio, lam0, it_ref, jt_ref, far_ref, q_ref, k_ref, v_ref, bias_ref, lq_ref, sub_ref,
               o_ref, qs_ref, sa_ref, sb_ref, m_ref, l_ref, acc_ref):
    step = pl.program_id(2)
    i = it_ref[step]
    j = jt_ref[step]
    last = (i + 1) * ratio - 1 if causal else pl.num_programs(2) - 2
    first_near = i * ratio - 1 if causal else 0
    tk = k_ref.shape[0]
    dk = HEAD_DIM
    tq = q_ref.shape[0]
    rb = 256 if tq % 256 == 0 else tq
    units = [(mp, slice(r * rb, (r + 1) * rb)) for r in range(tq // rb) for mp in range(2)]

    @pl.when(j == 0)
    def _():
        q = q_ref[...] * (dk ** -0.5 * LOG2E)
        qs_ref[0] = q[:, :dk].astype(BF16)
        qs_ref[1] = q[:, dk:].astype(BF16)
        m_ref[...] = jnp.full(m_ref.shape, NEG_INF, F32)
        l_ref[...] = jnp.zeros(l_ref.shape, F32)
        acc_ref[...] = jnp.zeros(acc_ref.shape, F32)

    s_bufs = (sa_ref, sb_ref)

    def scores(slot, mp, rows):
        s_bufs[slot][mp, rows, :] = _dot_nt(qs_ref[mp, rows, :], k_ref[:, mp * dk:(mp + 1) * dk])

    def softmax(slot, mp, rows, bias, shift):
        s = s_bufs[slot][mp, rows, :]
        if bias is not None:
            s = s + bias[rows]
        m_prev = m_ref[mp, rows, :]
        m_next = jnp.maximum(m_prev, jnp.max(s, axis=1, keepdims=True) + shift)
        p = jnp.exp2(s - jnp.concatenate([m_next - shift] * (tk // LANES), axis=1))
        alpha = jnp.exp2(m_prev - m_next)
        l_ref[mp, rows, :] = alpha * l_ref[mp, rows, :] + jnp.sum(p, axis=1, keepdims=True)
        m_ref[mp, rows, :] = m_next
        return p.astype(BF16), alpha

    def values(mp, rows, p16, alpha):
        acc_ref[mp, rows, :] = (acc_ref[mp, rows, :] * jnp.concatenate([alpha] * (2 * dk // LANES), axis=1)
                                + _dot(p16, v_ref[...]))

    def step_body(cur, prev, bias, shift):
        pending = None
        for mp, rows in units:
            probs = softmax(prev, mp, rows, bias, shift) if prev is not None else None
            if cur is not None:
                scores(cur, mp, rows)
            if pending is not None:
                values(*pending)
            pending = (mp, rows, *probs) if probs is not None else None
        if pending is not None:
            values(*pending)

    @pl.when(j == 0)
    def _():
        step_body(0, None, None, 0.0)

    for cur in range(2):
        prev = 1 - cur
        par = (j % 2) == cur
        if causal:
            @pl.when(par & (j >= 1) & (j - 1 < first_near))
            def _():
                step_body(cur, prev, None, far_ref[pl.program_id(1)])

        @pl.when(par & (j >= 1) & (j - 1 >= first_near) & (j <= last))
        def _():
            step_body(cur, prev, bias_ref[...], 0.0)

        @pl.when(par & (j == last + 1))
        def _():
            step_body(None, prev, bias_ref[...], 0.0)

    @pl.when(j == last + 1)
    def _():
        lq = lq_ref[...]
        lam = (jnp.exp(jnp.sum(lq[0:1] * lq[1:2], axis=1, keepdims=True))
               - jnp.exp(jnp.sum(lq[2:3] * lq[3:4], axis=1, keepdims=True)) + lam0)
        reps = 2 * dk // LANES
        o0 = acc_ref[0] / jnp.concatenate([l_ref[0]] * reps, axis=1)
        o1 = acc_ref[1] / jnp.concatenate([l_ref[1]] * reps, axis=1)
        o_ref[...] = _rms(o0 - lam * o1, sub_ref[...]) * (1.0 - lam0)


def _diff_attention(q, k, v, bias, far, lam_qk, subln, lam0, *, n_heads, tq, tk, causal):
    bsz, tq_len, _ = q.shape
    tk_len = k.shape[1]
    nq, nk = tq_len // tq, tk_len // tk
    dv = 2 * HEAD_DIM
    assert tq % tk == 0 if causal else nq == 1
    ratio = tq // tk if causal else 1

    def last_tile(i):
        return (i + 1) * ratio - 1 if causal else nk - 1

    pairs = [(i, j) for i in range(nq) for j in range(last_tile(i) + 2)]
    i_tab = jnp.asarray([p[0] for p in pairs], jnp.int32)
    j_tab = jnp.asarray([p[1] for p in pairs], jnp.int32)

    def q_map(b, h, s, it, jt):
        return (b, it[s], h)

    def k_map(b, h, s, it, jt):
        return (b, jnp.minimum(jt[s], last_tile(it[s])), h)

    def v_map(b, h, s, it, jt):
        return (b, jnp.clip(jt[s] - 1, 0, last_tile(it[s])), h)

    def bias_map(b, h, s, it, jt):
        if causal:
            return (h, jnp.clip(jt[s] - it[s] * ratio, 0, ratio), 0, 0)
        return (h, jnp.clip(jt[s] - 1, 0, nk - 1), 0, 0)

    const2 = lambda b, h, s, it, jt: (0, 0)
    return pl.pallas_call(
        functools.partial(_diff_body, causal, ratio, lam0),
        grid_spec=pltpu.PrefetchScalarGridSpec(
            num_scalar_prefetch=2,
            grid=(bsz, n_heads, len(pairs)),
            in_specs=[
                pl.BlockSpec(memory_space=pltpu.SMEM),
                pl.BlockSpec((None, tq, dv), q_map),
                pl.BlockSpec((None, tk, dv), k_map),
                pl.BlockSpec((None, tk, dv), v_map),
                pl.BlockSpec((None, None, tq, tk), bias_map),
                pl.BlockSpec((4, HEAD_DIM), const2),
                pl.BlockSpec((1, dv), const2),
            ],
            out_specs=pl.BlockSpec((None, tq, dv), q_map),
            scratch_shapes=[pltpu.VMEM((2, tq, HEAD_DIM), BF16),
                            pltpu.VMEM((2, tq, tk), F32), pltpu.VMEM((2, tq, tk), F32),
                            pltpu.VMEM((2, tq, LANES), F32), pltpu.VMEM((2, tq, LANES), F32),
                            pltpu.VMEM((2, tq, dv), F32)],
        ),
        out_shape=jax.ShapeDtypeStruct((bsz, tq_len, n_heads * dv), F32),
        compiler_params=_params("parallel", "parallel", "arbitrary"),
        name="diff_attn",
    )(i_tab, j_tab, far, q, k, v, bias, lam_qk, subln.reshape(1, dv))


def _rel_bucket(rel):
    nb = N_BUCKETS // 2
    max_exact = nb // 2
    n = jnp.abs(rel)
    large = max_exact + (jnp.log(jnp.maximum(n, 1).astype(F32) / max_exact)
                         / math.log(MAX_DISTANCE / max_exact) * (nb - max_exact)).astype(jnp.int32)
    large = jnp.minimum(large, nb - 1)
    return jnp.where(rel > 0, nb, 0) + jnp.where(n < max_exact, n, large)


def _bias_window(rel_table, q0, nq, k0, nk):
    rel = (k0 - q0) + jnp.concatenate([jnp.arange(nk + 1), jnp.arange(-(nq - 1), 0)])
    bucket = _rel_bucket(rel)[None]
    g = sum(jnp.where(bucket == b, rel_table[b][:, None], 0.0) for b in range(N_BUCKETS))
    span = nq + nk
    bias = jnp.tile(g, (1, nq))[:, :nq * (span - 1)].reshape(-1, nq, span - 1)[:, :, :nk]
    q_pos = q0 + jnp.arange(nq)
    k_pos = k0 + jnp.arange(nk)
    visible = (k_pos[None, :] // CHUNK) <= (q_pos[:, None] // CHUNK)
    return jnp.where(visible[None], bias, NEG_INF).astype(F32)


def _mix_out_body(o_ref, qm_ref, mk_ref, mv_ref, w_ref, h_ref, g_ref, out_ref, mix_ref):
    n_o = o_ref.shape[1]
    mix_ref[:, 0:n_o] = o_ref[...].astype(BF16)
    for hd in range(MEM_HEADS):
        sl = slice(hd * HEAD_DIM, (hd + 1) * HEAD_DIM)
        s = _dot_nt(qm_ref[:, sl].astype(BF16), mk_ref[:, sl]) * (HEAD_DIM ** -0.5)
        e = jnp.exp(s - jnp.max(s, axis=1, keepdims=True))
        p = e / jnp.sum(e, axis=1, keepdims=True)
        mix_ref[:, n_o + hd * HEAD_DIM:n_o + (hd + 1) * HEAD_DIM] = _dot(p.astype(BF16), mv_ref[:, sl]).astype(BF16)
    y = _dot(mix_ref[...], w_ref[...])
    out_ref[...] = h_ref[...] + _rms(y, g_ref[...])


def _mix_out(o, qm_src, qm_block, mem_k, mem_v, w_out, h, g):
    bsz, tlen, n_o = o.shape
    d = h.shape[2]
    mlen = mem_k.shape[1]
    tm = _row_tile(tlen, ROW_TILE)
    return pl.pallas_call(
        _mix_out_body,
        grid=(bsz, tlen // tm),
        in_specs=[
            pl.BlockSpec((None, tm, n_o), lambda b, i: (b, i, 0)),
            pl.BlockSpec((None, tm, MEM_WIDTH), lambda b, i: (b, i, qm_block)),
            pl.BlockSpec((None, mlen, MEM_WIDTH), lambda b, i: (b, 0, 0)),
            pl.BlockSpec((None, mlen, MEM_WIDTH), lambda b, i: (b, 0, 0)),
            pl.BlockSpec((d, d), lambda b, i: (0, 0)),
            pl.BlockSpec((None, tm, d), lambda b, i: (b, i, 0)),
            pl.BlockSpec((1, d), lambda b, i: (0, 0)),
        ],
        out_specs=pl.BlockSpec((None, tm, d), lambda b, i: (b, i, 0)),
        out_shape=jax.ShapeDtypeStruct(h.shape, F32),
        scratch_shapes=[pltpu.VMEM((tm, d), BF16)],
        compiler_params=_params("parallel", "parallel"),
        name="mix_out",
    )(o, qm_src, mem_k, mem_v, w_out, h, g.reshape(1, d))


def _lambda_init(layer_idx):
    return 0.8 - 0.6 * math.exp(-0.3 * layer_idx)


def _prep_weights(norm_gains, ffn_gate_up, ffn_down, w_out, w_mem_kv, w_in_a, conv_w_a, w_kv, w_in_b):
    d = w_out.shape[1]
    n_a = w_in_a.shape[0]
    conv_dim = conv_w_a.shape[2]
    mix = d - MEM_WIDTH
    n_heads = mix // HEAD_DIM
    qkvz = conv_dim + mix
    w = {
        "gu": ffn_gate_up.astype(BF16), "down": ffn_down.astype(BF16), "out": w_out.astype(BF16),
        "mem_k": w_mem_kv[:, :, :MEM_WIDTH].astype(BF16), "mem_v": w_mem_kv[:, :, MEM_WIDTH:].astype(BF16),
        "in_a": jnp.concatenate([w_in_a[:, :, :qkvz], w_in_a[:, :, qkvz + 2 * n_heads:]], axis=2).astype(BF16),
        "in_a_ba": jnp.pad(w_in_a[:, :, qkvz:qkvz + 2 * n_heads],
                           ((0, 0), (0, 0), (0, LANES - 2 * n_heads))).astype(BF16),
        "kv_k": w_kv[:, :mix].astype(BF16), "kv_v": w_kv[:, mix:].astype(BF16),
        "in_b": w_in_b.astype(BF16),
    }
    return w, n_a, n_heads, mix


def _run_group(x, mem_k, mem_v, conv_states, delta_states, past_k, past_v, q_pos0, W, P, *, blocked):
    w, n_a, n_heads, mix = W
    bsz, tlen, d = x.shape
    rows = bsz * tlen
    depth = P["norm_gains"].shape[0]
    diff_heads = mix // (2 * HEAD_DIM)
    h = x.reshape(rows, d)
    new_conv, new_delta = [], []
    k_new = v_new = None
    for l in range(depth):
        ng = P["norm_gains"][l]
        if l == n_a:
            k_new, v_new, k16, v16 = _proj(h, P["kv_norm"], [w["kv_k"], w["kv_v"]], tm=512, tn=768,
                                           with_bf16=True, split_first=2, name="proj_kv")
        h = _ffn(h, ng[0], ng[1], w["gu"], w["down"], l, 0)
        if l < n_a:
            proj, ba = _proj(h, ng[2], [w["in_a"][l]], w["in_a_ba"][l], tm=1024, tn=1664, name="proj_a")
            n_proj = proj.shape[1]
            proj = proj.reshape(bsz, tlen, n_proj)
            conv_dim = 3 * mix
            new_conv.append(proj[:, tlen - (CONV_WIDTH - 1):, :conv_dim])
            conv_buf = jnp.pad(conv_states[l], ((0, 0), (SUBLANES - (CONV_WIDTH - 1), 0), (0, 0)))
            o, s_out = _gdn(proj, ba.reshape(bsz, tlen, LANES), conv_buf, P["conv_w_a"][l], P["a_log"][l],
                            P["dt_bias"][l], P["onorm_a"][l], delta_states[l], nc=GDN_CHUNKS if tlen % (GDN_CHUNKS * CHUNK) == 0 else 1, hg=GDN_HEADS)
            new_delta.append(s_out)
            qm_src, qm_block = proj, (conv_dim + mix) // MEM_WIDTH
        else:
            jb = l - n_a
            (proj,) = _proj(h, ng[2], [w["in_b"][jb]], tm=512, tn=2048, name="proj_b")
            proj = proj.reshape(bsz, tlen, d)
            lam0 = _lambda_init(l)
            k3 = k16.reshape(bsz, tlen, mix)
            v3 = v16.reshape(bsz, tlen, mix)
            if blocked:
                tq = _row_tile(tlen, ATTN_Q_TILE)
                tk = _row_tile(tq, ATTN_K_TILE)
                band = tq // tk + 1
                near = _bias_window(P["rel_bias"], tq, tq, tq - tk, band * tk)
                bias = jnp.stack([near[:, :, n * tk:(n + 1) * tk] for n in range(band)], axis=1) * LOG2E
                assert tk >= MAX_DISTANCE and tq % CHUNK == 0
                far = P["rel_bias"][_rel_bucket(jnp.int32(-2 * MAX_DISTANCE))] * LOG2E
                o = _diff_attention(proj, k3, v3, bias, far, P["lambda_qk"][jb], P["subln_b"][jb], lam0,
                                    n_heads=diff_heads, tq=tq, tk=tk, causal=True)
            else:
                past = past_k.shape[1]
                total = past + tlen
                padded = -(-total // LANES) * LANES
                grow = lambda c, n: jnp.pad(jnp.concatenate([c.reshape(bsz, past, mix).astype(BF16), n], axis=1),
                                            ((0, 0), (0, padded - total), (0, 0)))
                k_pos = jnp.arange(padded)
                bias = _bias_window(P["rel_bias"], q_pos0, tlen, 0, padded)
                bias = jnp.where(k_pos[None, None, :] < total, bias, NEG_INF)[:, None] * LOG2E
                o = _diff_attention(proj, grow(past_k, k3), grow(past_v, v3), bias, jnp.zeros((diff_heads,), F32),
                                    P["lambda_qk"][jb], P["subln_b"][jb], lam0, n_heads=diff_heads, tq=tlen,
                                    tk=padded, causal=False)
            qm_src, qm_block = proj, mix // MEM_WIDTH
        h = _mix_out(o, qm_src, qm_block, mem_k[l], mem_v[l], w["out"][l], h.reshape(bsz, tlen, d), ng[3])
        h = _ffn(h.reshape(rows, d), ng[4], ng[5], w["gu"], w["down"], l, 1)
    return h.reshape(bsz, tlen, d), jnp.stack(new_conv), jnp.stack(new_delta), k_new, v_new


def kernel(x_prompt, x_sample, mem_prompt, cache_k, cache_v, cache_mem_k, cache_mem_v, state_delta, state_conv,
           norm_gains, ffn_gate_up, ffn_down, w_out, mem_norm, w_mem_kv, w_in_a, conv_w_a, a_log, dt_bias,
           onorm_a, kv_norm, w_kv, w_in_b, lambda_qk, subln_b, rel_bias):
    W = _prep_weights(norm_gains, ffn_gate_up, ffn_down, w_out, w_mem_kv, w_in_a, conv_w_a, w_kv, w_in_b)
    w, n_a, n_heads, mix = W
    P = {"norm_gains": norm_gains, "kv_norm": kv_norm, "conv_w_a": conv_w_a, "a_log": a_log, "dt_bias": dt_bias,
         "onorm_a": onorm_a, "lambda_qk": lambda_qk, "subln_b": subln_b, "rel_bias": rel_bias}
    depth = norm_gains.shape[0]
    bp, tp, d = x_prompt.shape
    bs, ts, _ = x_sample.shape
    mlen = mem_prompt.shape[1]
    diff_heads = mix // (2 * HEAD_DIM)

    mem_rows = mem_prompt.reshape(bp * mlen, d)
    mem_kv = [_proj(mem_rows, mem_norm[l], [w["mem_k"][l], w["mem_v"][l]], tm=256, tn=512, name="proj_mem")
              for l in range(depth)]
    mem_k_p = jnp.stack([kv[0] for kv in mem_kv]).reshape(depth, bp, mlen, MEM_HEADS, HEAD_DIM)
    mem_v_p = jnp.stack([kv[1] for kv in mem_kv]).reshape(depth, bp, mlen, MEM_HEADS, HEAD_DIM)
    conv0 = jnp.zeros((n_a, bp, CONV_WIDTH - 1, 3 * mix), F32)
    delta0 = jnp.zeros((n_a, bp, n_heads, HEAD_DIM, HEAD_DIM), F32)
    flat = lambda m: m.reshape(m.shape[0], m.shape[1], m.shape[2], MEM_WIDTH).astype(BF16)
    y_p, conv_p, delta_p, k_p, v_p = _run_group(
        x_prompt, flat(mem_k_p), flat(mem_v_p), conv0, delta0, None, None, 0, W, P, blocked=True)

    y_s, conv_s, delta_s, k_s, v_s = _run_group(
        x_sample, flat(cache_mem_k), flat(cache_mem_v), state_conv, state_delta, cache_k, cache_v,
        cache_k.shape[1], W, P, blocked=False)

    shape_k = lambda a, b, t: a.reshape(b, t, diff_heads, 2, HEAD_DIM)
    shape_v = lambda a, b, t: a.reshape(b, t, diff_heads, 2 * HEAD_DIM)
    return (y_p, y_s, delta_p, conv_p, shape_k(k_p, bp, tp), shape_v(v_p, bp, tp), mem_k_p, mem_v_p,
            delta_s, conv_s, shape_k(k_s, bs, ts), shape_v(v_s, bs, ts))
```

```python
import functools
import math

import jax
import jax.numpy as jnp
from jax import lax
from jax.experimental import pallas as pl
from jax.experimental.pallas import tpu as pltpu

F32 = jnp.float32
BF16 = jnp.bfloat16

EPS = 1e-6
CHUNK = 64
HEAD_DIM = 128
MEM_HEADS = 4
MEM_WIDTH = MEM_HEADS * HEAD_DIM
CONV_WIDTH = 4
N_BUCKETS = 32
MAX_DISTANCE = 128
NEG_INF = -1e30
LOG2E = math.log2(math.e)

LANES = 128
SUBLANES = 8
VMEM_LIMIT = 56 * 1024 * 1024

ROW_TILE = 512
FFN_ROW_TILE = 1024
FFN_VMEM_LIMIT = 60 * 1024 * 1024
ATTN_Q_TILE = 1024
ATTN_K_TILE = 1024
GDN_CHUNKS = 4
GDN_HEADS = 4


def _params(*sem):
    return pltpu.CompilerParams(dimension_semantics=sem, vmem_limit_bytes=VMEM_LIMIT)


def _dot(a, b, precision=None):
    return jnp.dot(a, b, preferred_element_type=F32, precision=precision)


def _dot_nt(a, b):
    return lax.dot_general(a, b, (((1,), (1,)), ((), ())), preferred_element_type=F32)


def _dot_tn(a, b):
    return lax.dot_general(a, b, (((0,), (0,)), ((), ())), preferred_element_type=F32)


def _rms(x, gain):
    return x * lax.rsqrt(jnp.mean(x * x, axis=-1, keepdims=True) + EPS) * gain


def _silu(x):
    return x * jax.nn.sigmoid(x)


def _row_tile(rows, want):
    t = min(want, rows)
    assert rows % t == 0, (rows, t)
    return t


def _ffn_body(nf, x_ref, g0_ref, g1_ref, wg_ref, wu_ref, wd_ref, o_ref, xn_ref):
    f = pl.program_id(1)

    def partial_product():
        xn = xn_ref[...]
        gate = _dot(xn, wg_ref[...])
        up = _dot(xn, wu_ref[...])
        return _dot((_silu(gate) * up).astype(BF16), wd_ref[...])

    def finish(acc):
        o_ref[...] = x_ref[...] + 0.5 * _rms(acc, g1_ref[...])

    @pl.when(f == 0)
    def _():
        xn_ref[...] = _rms(x_ref[...], g0_ref[...]).astype(BF16)
        if nf == 1:
            finish(partial_product())
        else:
            o_ref[...] = partial_product()

    if nf > 1:
        @pl.when((f > 0) & (f < nf - 1))
        def _():
            o_ref[...] += partial_product()

        @pl.when(f == nf - 1)
        def _():
            finish(o_ref[...] + partial_product())


def _ffn(x, g0, g1, w_gu, w_down, layer, half, *, tf=512):
    rows, d = x.shape
    dff = w_down.shape[2]
    tm = _row_tile(rows, FFN_ROW_TILE)
    tf = _row_tile(dff, tf)
    nf = dff // tf
    return pl.pallas_call(
        functools.partial(_ffn_body, nf),
        grid=(rows // tm, nf),
        in_specs=[
            pl.BlockSpec((tm, d), lambda i, f: (i, 0), pipeline_mode=pl.Buffered(1)),
            pl.BlockSpec((1, d), lambda i, f: (0, 0)),
            pl.BlockSpec((1, d), lambda i, f: (0, 0)),
            pl.BlockSpec((None, None, d, tf), lambda i, f: (layer, half, 0, f)),
            pl.BlockSpec((None, None, d, tf), lambda i, f: (layer, half, 0, f + nf)),
            pl.BlockSpec((None, None, tf, d), lambda i, f: (layer, half, f, 0)),
        ],
        out_specs=pl.BlockSpec((tm, d), lambda i, f: (i, 0)),
        out_shape=jax.ShapeDtypeStruct((rows, d), F32),
        scratch_shapes=[pltpu.VMEM((tm, d), BF16)],
        compiler_params=pltpu.CompilerParams(dimension_semantics=("parallel", "arbitrary"),
                                             vmem_limit_bytes=FFN_VMEM_LIMIT),
        name="ffn",
    )(x, g0.reshape(1, d), g1.reshape(1, d), w_gu, w_gu, w_down)


def _proj_body(n_main, has_side, with_bf16, *refs):
    x_ref, g_ref = refs[0], refs[1]
    w_refs = refs[2:2 + n_main]
    pos = 2 + n_main
    ws_ref = refs[pos] if has_side else None
    pos += int(has_side)
    o_refs = refs[pos:pos + n_main]
    pos += n_main
    o16_refs = refs[pos:pos + n_main] if with_bf16 else [None] * n_main
    pos += n_main * int(with_bf16)
    os_ref = refs[pos] if has_side else None
    pos += int(has_side)
    xn_ref = refs[pos]

    @pl.when(pl.program_id(1) == 0)
    def _():
        xn_ref[...] = _rms(x_ref[...], g_ref[...]).astype(BF16)
        if has_side:
            os_ref[...] = _dot(xn_ref[...], ws_ref[...])

    xn = xn_ref[...]
    for w_ref, o_ref, o16_ref in zip(w_refs, o_refs, o16_refs):
        y = _dot(xn, w_ref[...])
        if len(o_ref.shape) == 2:
            o_ref[...] = y
        else:
            _, nh, nm, dh = o_ref.shape
            for hh in range(nh):
                for mm in range(nm):
                    c0 = (hh * nm + mm) * dh
                    o_ref[:, hh, mm, :] = y[:, c0:c0 + dh]
        if with_bf16:
            o16_ref[...] = y.astype(BF16)


def _proj(x, g, w_mains, w_side=None, *, tm, tn, with_bf16=False, split_first=None, name="proj"):
    rows, d = x.shape
    n = w_mains[0].shape[1]
    assert all(w.shape == (d, n) for w in w_mains)
    tm = _row_tile(rows, tm)
    tn = _row_tile(n, tn)
    has_side = w_side is not None
    in_specs = [pl.BlockSpec((tm, d), lambda i, j: (i, 0)), pl.BlockSpec((1, d), lambda i, j: (0, 0))]
    in_specs += [pl.BlockSpec((d, tn), lambda i, j: (0, j)) for _ in w_mains]
    out_specs = [pl.BlockSpec((tm, tn), lambda i, j: (i, j)) for _ in w_mains]
    out_shape = [jax.ShapeDtypeStruct((rows, n), F32) for _ in w_mains]
    if split_first is not None:
        nm = split_first
        group = nm * HEAD_DIM
        out_specs[0] = pl.BlockSpec((tm, tn // group, nm, HEAD_DIM), lambda i, j: (i, j, 0, 0))
        out_shape[0] = jax.ShapeDtypeStruct((rows, n // group, nm, HEAD_DIM), F32)
    if with_bf16:
        out_specs += [pl.BlockSpec((tm, tn), lambda i, j: (i, j)) for _ in w_mains]
        out_shape += [jax.ShapeDtypeStruct((rows, n), BF16) for _ in w_mains]
    args = [x, g.reshape(1, d), *w_mains]
    if has_side:
        ns = w_side.shape[1]
        in_specs.append(pl.BlockSpec((d, ns), lambda i, j: (0, 0)))
        out_specs.append(pl.BlockSpec((tm, ns), lambda i, j: (i, 0)))
        out_shape.append(jax.ShapeDtypeStruct((rows, ns), F32))
        args.append(w_side)
    return pl.pallas_call(
        functools.partial(_proj_body, len(w_mains), has_side, with_bf16),
        grid=(rows // tm, n // tn),
        in_specs=in_specs,
        out_specs=out_specs,
        out_shape=out_shape,
        scratch_shapes=[pltpu.VMEM((tm, d), BF16)],
        compiler_params=_params("parallel", "arbitrary"),
        name=name,
    )(*args)


def _gdn_body(nc, hg, n_heads, q_ref, k_ref, v_ref, z_ref, ba_ref, cq_ref, ck_ref, cv_ref, wq_ref, wk_ref, wv_ref,
              alog_ref, dtb_ref, onorm_ref, s0_ref, o_ref, sout_ref, xs_ref, state_ref):
    head0 = pl.program_id(1) * hg
    t = pl.program_id(2)
    tr = nc * CHUNK
    halo = SUBLANES

    @pl.when(t == 0)
    def _():
        xs_ref[0, 0:halo, :] = cq_ref[...]
        xs_ref[1, 0:halo, :] = ck_ref[...]
        xs_ref[2, 0:halo, :] = cv_ref[...]
        state_ref[...] = s0_ref[...]

    def conv(c, x_ref, w_ref):
        xs_ref[c, halo:halo + tr, :] = x_ref[...]
        w = w_ref[...]
        y = sum(xs_ref[c, halo - 3 + j:halo - 3 + j + tr, :] * w[j:j + 1, :] for j in range(CONV_WIDTH))
        xs_ref[c, 0:halo, :] = xs_ref[c, tr:tr + halo, :]
        return _silu(y)

    q_all = conv(0, q_ref, wq_ref)
    k_all = conv(1, k_ref, wk_ref)
    v_all = conv(2, v_ref, wv_ref)

    ba = ba_ref[...]
    g_all =-jnp.exp(alog_ref[...]) * jax.nn.softplus(ba + dtb_ref[...])

    row = lax.broadcasted_iota(jnp.int32, (CHUNK, 2 * CHUNK), 0)
    lane2 = lax.broadcasted_iota(jnp.int32, (CHUNK, 2 * CHUNK), 1)
    col = lane2 & (CHUNK - 1)
    left = lane2 < CHUNK
    incl = row >= col
    zeros_c = jnp.zeros((CHUNK, HEAD_DIM), F32)
    chunks = [slice(c * CHUNK, (c + 1) * CHUNK) for c in range(nc)]

    ctx = []
    for hd, sl in [(hd, sl) for hd in range(hg) for sl in chunks]:
        hs = slice(hd * HEAD_DIM, (hd + 1) * HEAD_DIM)
        qc, kc, vc = q_all[sl, hs], k_all[sl, hs], v_all[sl, hs]
        qc = qc * lax.rsqrt(jnp.sum(qc * qc, axis=-1, keepdims=True) + EPS) * (HEAD_DIM ** -0.5)
        kc = kc * lax.rsqrt(jnp.sum(kc * kc, axis=-1, keepdims=True) + EPS)
        bc = jax.nn.sigmoid(jnp.sum(jnp.where(lane2 == head0 + hd, ba[sl], 0.0), axis=1, keepdims=True))
        gc = jnp.sum(jnp.where(lane2 == head0 + hd + n_heads, g_all[sl], 0.0), axis=1, keepdims=True)
        g_row = jnp.sum(jnp.where(row == col, gc, 0.0), axis=0, keepdims=True)
        cum_col = jnp.sum(jnp.where(left, jnp.where(incl, g_row, 0.0), 0.0), axis=1, keepdims=True)
        cum_row = jnp.sum(jnp.where(row <= col, gc, 0.0), axis=0, keepdims=True)
        g_last = jnp.sum(gc, axis=0, keepdims=True)
        decay = jnp.where(incl, jnp.exp(jnp.where(incl, cum_col - cum_row, 0.0)), 0.0)
        kb = kc * bc
        kk_qk = _dot_nt(jnp.concatenate([kb, qc], axis=0).astype(BF16),
                        jnp.concatenate([kc, kc], axis=0).astype(BF16))
        lower = jnp.where(row > col, kk_qk[:CHUNK] * decay, 0.0)
        a_intra = jnp.where(incl, kk_qk[CHUNK:] * decay, 0.0)
        e_cum = jnp.exp(cum_col)
        ctx.append(dict(
            hd=hd, sl=sl, hs=hs,
            x=jnp.where(left, jnp.where(row == col, 1.0, 0.0), -lower),
            y=jnp.where(left, -lower, jnp.where(row == col, 1.0, 0.0)),
            rhs=jnp.concatenate([vc * bc, kb * e_cum], axis=1),
            a=a_intra.astype(BF16), qg=qc * e_cum,
            kd=(kc * jnp.exp(g_last - cum_col)).astype(BF16), gl=jnp.exp(g_last)))

    def split(a):
        hi16 = a.astype(BF16)
        hi = hi16.astype(F32)
        lo = a - hi
        return hi16, hi, lo.astype(BF16), lo

    zeros_rhs = jnp.zeros((CHUNK, 4 * CHUNK), BF16)
    for _ in range(6):
        for cx in ctx:
            xh16, xh, xl16, _ = split(cx["x"])
            yh16, yh, yl16, yl = split(cx["y"])
            lhs = jnp.concatenate([jnp.where(left, yl, xh), jnp.where(left, yh, 0.0)], axis=1).astype(BF16)
            top = jnp.concatenate([xh16, yh16], axis=1)
            rhs = jnp.concatenate([top, jnp.concatenate([xl16, yl16], axis=1), top, zeros_rhs], axis=0)
            prod = _dot(lhs, rhs)
            cx["x"] = prod[:, :2 * CHUNK] + jnp.where(left, cx["x"], 0.0)
            cx["y"] = prod[:, 2 * CHUNK:] + jnp.where(left, 0.0, cx["y"])

    for cx in ctx:
        t16 = jnp.where(left, cx["x"], 0.0).astype(BF16)
        uw = _dot(t16, jnp.concatenate([cx["rhs"], jnp.zeros_like(cx["rhs"])], axis=0).astype(BF16))
        cx["u"] = uw[:, :HEAD_DIM]
        cx["wq"] = jnp.concatenate([uw[:, HEAD_DIM:], cx["qg"]], axis=0).astype(BF16)

    states = [state_ref[hd] for hd in range(hg)]
    for cx in sorted(ctx, key=lambda cx: (cx["sl"].start, cx["hd"])):
        hd, sl, hs = cx["hd"], cx["sl"], cx["hs"]
        ws = _dot(cx["wq"], states[hd].astype(BF16))
        v_new = cx["u"] - ws[:CHUNK]
        o = ws[CHUNK:] + _dot(cx["a"], jnp.concatenate([v_new, zeros_c], axis=0).astype(BF16))
        states[hd] = states[hd] * cx["gl"] + _dot_tn(cx["kd"], v_new.astype(BF16))
        o_ref[sl, hs] = _rms(o, onorm_ref[...]) * _silu(z_ref[sl, hs])

    for hd in range(hg):
        state_ref[hd] = states[hd]
        sout_ref[hd] = states[hd]


def _gdn(proj, ba, conv_buf, conv_w, a_log, dt_bias, onorm, s0, *, nc, hg):
    bsz, tlen, _ = proj.shape
    n_heads = s0.shape[1]
    tr = nc * CHUNK
    assert tlen % tr == 0 and n_heads % hg == 0
    nt = tlen // tr
    ngroups = n_heads // hg
    width = hg * HEAD_DIM
    pad = lambda a: jnp.zeros((1, LANES), F32).at[0, n_heads:2 * n_heads].set(a)

    def col(c):
        return pl.BlockSpec((None, tr, width), lambda b, h, t: (b, t, c * ngroups + h))

    def buf(c):
        return pl.BlockSpec((None, SUBLANES, width), lambda b, h, t: (b, 0, c * ngroups + h))

    def tap(c):
        return pl.BlockSpec((CONV_WIDTH, width), lambda b, h, t: (0, c * ngroups + h))

    vec = pl.BlockSpec((1, LANES), lambda b, h, t: (0, 0))
    state_spec = pl.BlockSpec((None, hg, HEAD_DIM, HEAD_DIM), lambda b, h, t: (b, h, 0, 0))
    return pl.pallas_call(
        functools.partial(_gdn_body, nc, hg, n_heads),
        grid=(bsz, ngroups, nt),
        in_specs=[col(0), col(1), col(2), col(3),
                  pl.BlockSpec((None, tr, LANES), lambda b, h, t: (b, t, 0)),
                  buf(0), buf(1), buf(2), tap(0), tap(1), tap(2), vec, vec, vec, state_spec],
        out_specs=[pl.BlockSpec((None, tr, width), lambda b, h, t: (b, t, h)), state_spec],
        out_shape=[jax.ShapeDtypeStruct((bsz, tlen, n_heads * HEAD_DIM), F32),
                   jax.ShapeDtypeStruct(s0.shape, F32)],
        scratch_shapes=[pltpu.VMEM((3, tr + SUBLANES, width), F32), pltpu.VMEM((hg, HEAD_DIM, HEAD_DIM), F32)],
        compiler_params=_params("parallel", "parallel", "arbitrary"),
        name="gdn",
    )(proj, proj, proj, proj, ba, conv_buf, conv_buf, conv_buf, conv_w, conv_w, conv_w,
      pad(a_log), pad(dt_bias), onorm.reshape(1, HEAD_DIM), s0)


def _diff_body(causal, ratio, lam0, it_ref, jt_ref, far_ref, q_ref, k_ref, v_ref, bias_ref, lq_ref, sub_ref,
               o_ref, qs_ref, sa_ref, sb_ref, m_ref, l_ref, acc_ref):
    step = pl.program_id(2)
    i = it_ref[step]
    j = jt_ref[step]
    last = (i + 1) * ratio - 1 if causal else pl.num_programs(2) - 2
    first_near = i * ratio - 1 if causal else 0
    tk = k_ref.shape[0]
    dk = HEAD_DIM
    tq = q_ref.shape[0]
    rb = 256 if tq % 256 == 0 else tq
    units = [(mp, slice(r * rb, (r + 1) * rb)) for r in range(tq // rb) for mp in range(2)]

    @pl.when(j == 0)
    def _():
        q = q_ref[...] * (dk ** -0.5 * LOG2E)
        qs_ref[0] = q[:, :dk].astype(BF16)
        qs_ref[1] = q[:, dk:].astype(BF16)
        m_ref[...] = jnp.full(m_ref.shape, NEG_INF, F32)
        l_ref[...] = jnp.zeros(l_ref.shape, F32)
        acc_ref[...] = jnp.zeros(acc_ref.shape, F32)

    s_bufs = (sa_ref, sb_ref)

    def scores(slot, mp, rows):
        s_bufs[slot][mp, rows, :] = _dot_nt(qs_ref[mp, rows, :], k_ref[:, mp * dk:(mp + 1) * dk])

    def softmax(slot, mp, rows, bias, shift):
        s = s_bufs[slot][mp, rows, :]
        if bias is not None:
            s = s + bias[rows]
        m_prev = m_ref[mp, rows, :]
        m_next = jnp.maximum(m_prev, jnp.max(s, axis=1, keepdims=True) + shift)
        p = jnp.exp2(s - jnp.concatenate([m_next - shift] * (tk // LANES), axis=1))
        alpha = jnp.exp2(m_prev - m_next)
        l_ref[mp, rows, :] = alpha * l_ref[mp, rows, :] + jnp.sum(p, axis=1, keepdims=True)
        m_ref[mp, rows, :] = m_next
        return p.astype(BF16), alpha

    def values(mp, rows, p16, alpha):
        acc_ref[mp, rows, :] = (acc_ref[mp, rows, :] * jnp.concatenate([alpha] * (2 * dk // LANES), axis=1)
                                + _dot(p16, v_ref[...]))

    def step_body(cur, prev, bias, shift):
        pending = None
        for mp, rows in units:
            probs = softmax(prev, mp, rows, bias, shift) if prev is not None else None
            if cur is not None:
                scores(cur, mp, rows)
            if pending is not None:
                values(*pending)
            pending = (mp, rows, *probs) if probs is not None else None
        if pending is not None:
            values(*pending)

    @pl.when(j == 0)
    def _():
        step_body(0, None, None, 0.0)

    for cur in range(2):
        prev = 1 - cur
        par = (j % 2) == cur
        if causal:
            @pl.when(par & (j >= 1) & (j - 1 < first_near))
            def _():
                step_body(cur, prev, None, far_ref[pl.program_id(1)])

        @pl.when(par & (j >= 1) & (j - 1 >= first_near) & (j <= last))
        def _():
            step_body(cur, prev, bias_ref[...], 0.0)

        @pl.when(par & (j == last + 1))
        def _():
            step_body(None, prev, bias_ref[...], 0.0)

    @pl.when(j == last + 1)
    def _():
        lq = lq_ref[...]
        lam = (jnp.exp(jnp.sum(lq[0:1] * lq[1:2], axis=1, keepdims=True))
               - jnp.exp(jnp.sum(lq[2:3] * lq[3:4], axis=1, keepdims=True)) + lam0)
        reps = 2 * dk // LANES
        o0 = acc_ref[0] / jnp.concatenate([l_ref[0]] * reps, axis=1)
        o1 = acc_ref[1] / jnp.concatenate([l_ref[1]] * reps, axis=1)
        o_ref[...] = _rms(o0 - lam * o1, sub_ref[...]) * (1.0 - lam0)


def _diff_attention(q, k, v, bias, far, lam_qk, subln, lam0, *, n_heads, tq, tk, causal):
    bsz, tq_len, _ = q.shape
    tk_len = k.shape[1]
    nq, nk = tq_len // tq, tk_len // tk
    dv = 2 * HEAD_DIM
    assert tq % tk == 0 if causal else nq == 1
    ratio = tq // tk if causal else 1

    def last_tile(i):
        return (i + 1) * ratio - 1 if causal else nk - 1

    pairs = [(i, j) for i in range(nq) for j in range(last_tile(i) + 2)]
    i_tab = jnp.asarray([p[0] for p in pairs], jnp.int32)
    j_tab = jnp.asarray([p[1] for p in pairs], jnp.int32)

    def q_map(b, h, s, it, jt):
        return (b, it[s], h)

    def k_map(b, h, s, it, jt):
        return (b, jnp.minimum(jt[s], last_tile(it[s])), h)

    def v_map(b, h, s, it, jt):
        return (b, jnp.clip(jt[s] - 1, 0, last_tile(it[s])), h)

    def bias_map(b, h, s, it, jt):
        if causal:
            return (h, jnp.clip(jt[s] - it[s] * ratio, 0, ratio), 0, 0)
        return (h, jnp.clip(jt[s] - 1, 0, nk - 1), 0, 0)

    const2 = lambda b, h, s, it, jt: (0, 0)
    return pl.pallas_call(
        functools.partial(_diff_body, causal, ratio, lam0),
        grid_spec=pltpu.PrefetchScalarGridSpec(
            num_scalar_prefetch=2,
            grid=(bsz, n_heads, len(pairs)),
            in_specs=[
                pl.BlockSpec(memory_space=pltpu.SMEM),
                pl.BlockSpec((None, tq, dv), q_map),
                pl.BlockSpec((None, tk, dv), k_map),
                pl.BlockSpec((None, tk, dv), v_map),
                pl.BlockSpec((None, None, tq, tk), bias_map),
                pl.BlockSpec((4, HEAD_DIM), const2),
                pl.BlockSpec((1, dv), const2),
            ],
            out_specs=pl.BlockSpec((None, tq, dv), q_map),
            scratch_shapes=[pltpu.VMEM((2, tq, HEAD_DIM), BF16),
                            pltpu.VMEM((2, tq, tk), F32), pltpu.VMEM((2, tq, tk), F32),
                            pltpu.VMEM((2, tq, LANES), F32), pltpu.VMEM((2, tq, LANES), F32),
                            pltpu.VMEM((2, tq, dv), F32)],
        ),
        out_shape=jax.ShapeDtypeStruct((bsz, tq_len, n_heads * dv), F32),
        compiler_params=_params("parallel", "parallel", "arbitrary"),
        name="diff_attn",
    )(i_tab, j_tab, far, q, k, v, bias, lam_qk, subln.reshape(1, dv))


def _rel_bucket(rel):
    nb = N_BUCKETS // 2
    max_exact = nb // 2
    n = jnp.abs(rel)
    large = max_exact + (jnp.log(jnp.maximum(n, 1).astype(F32) / max_exact)
                         / math.log(MAX_DISTANCE / max_exact) * (nb - max_exact)).astype(jnp.int32)
    large = jnp.minimum(large, nb - 1)
    return jnp.where(rel > 0, nb, 0) + jnp.where(n < max_exact, n, large)


def _bias_window(rel_table, q0, nq, k0, nk):
    blk = min(LANES, nq)
    assert nq % blk == 0 and nk % blk == 0 and blk % CHUNK == 0
    na = nq // blk
    q_pos = q0 + jnp.arange(blk)[:, None]
    k_pos = k0 - (na - 1) * blk + jnp.arange(nk + (na - 1) * blk)[None, :]
    bucket = _rel_bucket(k_pos - q_pos)[None]
    bias = sum(jnp.where(bucket == b, rel_table[b][:, None, None], 0.0) for b in range(N_BUCKETS))
    wide = jnp.where(((k_pos // CHUNK) <= (q_pos // CHUNK))[None], bias, NEG_INF).astype(F32)
    return jnp.concatenate([wide[:, :, (na - 1 - a) * blk:(na - 1 - a) * blk + nk] for a in range(na)], axis=1)


def _mix_out_body(o_ref, qm_ref, mk_ref, mv_ref, w_ref, h_ref, g_ref, out_ref, mix_ref):
    n_o = o_ref.shape[1]
    mix_ref[:, 0:n_o] = o_ref[...].astype(BF16)
    for hd in range(MEM_HEADS):
        sl = slice(hd * HEAD_DIM, (hd + 1) * HEAD_DIM)
        s = _dot_nt(qm_ref[:, sl].astype(BF16), mk_ref[:, sl]) * (HEAD_DIM ** -0.5)
        e = jnp.exp(s - jnp.max(s, axis=1, keepdims=True))
        p = e / jnp.sum(e, axis=1, keepdims=True)
        mix_ref[:, n_o + hd * HEAD_DIM:n_o + (hd + 1) * HEAD_DIM] = _dot(p.astype(BF16), mv_ref[:, sl]).astype(BF16)
    y = _dot(mix_ref[...], w_ref[...])
    out_ref[...] = h_ref[...] + _rms(y, g_ref[...])


def _mix_out(o, qm_src, qm_block, mem_k, mem_v, w_out, h, g):
    bsz, tlen, n_o = o.shape
    d = h.shape[2]
    mlen = mem_k.shape[1]
    tm = _row_tile(tlen, ROW_TILE)
    return pl.pallas_call(
        _mix_out_body,
        grid=(bsz, tlen // tm),
        in_specs=[
            pl.BlockSpec((None, tm, n_o), lambda b, i: (b, i, 0)),
            pl.BlockSpec((None, tm, MEM_WIDTH), lambda b, i: (b, i, qm_block)),
            pl.BlockSpec((None, mlen, MEM_WIDTH), lambda b, i: (b, 0, 0)),
            pl.BlockSpec((None, mlen, MEM_WIDTH), lambda b, i: (b, 0, 0)),
            pl.BlockSpec((d, d), lambda b, i: (0, 0)),
            pl.BlockSpec((None, tm, d), lambda b, i: (b, i, 0)),
            pl.BlockSpec((1, d), lambda b, i: (0, 0)),
        ],
        out_specs=pl.BlockSpec((None, tm, d), lambda b, i: (b, i, 0)),
        out_shape=jax.ShapeDtypeStruct(h.shape, F32),
        scratch_shapes=[pltpu.VMEM((tm, d), BF16)],
        compiler_params=_params("parallel", "parallel"),
        name="mix_out",
    )(o, qm_src, mem_k, mem_v, w_out, h, g.reshape(1, d))


def _lambda_init(layer_idx):
    return 0.8 - 0.6 * math.exp(-0.3 * layer_idx)


def _prep_weights(norm_gains, ffn_gate_up, ffn_down, w_out, w_mem_kv, w_in_a, conv_w_a, w_kv, w_in_b):
    d = w_out.shape[1]
    n_a = w_in_a.shape[0]
    conv_dim = conv_w_a.shape[2]
    mix = d - MEM_WIDTH
    n_heads = mix // HEAD_DIM
    qkvz = conv_dim + mix
    w = {
        "gu": ffn_gate_up.astype(BF16), "down": ffn_down.astype(BF16), "out": w_out.astype(BF16),
        "mem_k": w_mem_kv[:, :, :MEM_WIDTH].astype(BF16), "mem_v": w_mem_kv[:, :, MEM_WIDTH:].astype(BF16),
        "in_a": jnp.concatenate([w_in_a[:, :, :qkvz], w_in_a[:, :, qkvz + 2 * n_heads:]], axis=2).astype(BF16),
        "in_a_ba": jnp.pad(w_in_a[:, :, qkvz:qkvz + 2 * n_heads],
                           ((0, 0), (0, 0), (0, LANES - 2 * n_heads))).astype(BF16),
        "kv_k": w_kv[:, :mix].astype(BF16), "kv_v": w_kv[:, mix:].astype(BF16),
        "in_b": w_in_b.astype(BF16),
    }
    return w, n_a, n_heads, mix


def _run_group(x, mem_k, mem_v, conv_states, delta_states, past_k, past_v, q_pos0, W, P, *, blocked):
    w, n_a, n_heads, mix = W
    bsz, tlen, d = x.shape
    rows = bsz * tlen
    depth = P["norm_gains"].shape[0]
    diff_heads = mix // (2 * HEAD_DIM)
    h = x.reshape(rows, d)
    new_conv, new_delta = [], []
    k_new = v_new = None
    for l in range(depth):
        ng = P["norm_gains"][l]
        if l == n_a:
            k_new, v_new, k16, v16 = _proj(h, P["kv_norm"], [w["kv_k"], w["kv_v"]], tm=512, tn=768,
                                           with_bf16=True, split_first=2, name="proj_kv")
        h = _ffn(h, ng[0], ng[1], w["gu"], w["down"], l, 0)
        if l < n_a:
            proj, ba = _proj(h, ng[2], [w["in_a"][l]], w["in_a_ba"][l], tm=1024, tn=1664, name="proj_a")
            n_proj = proj.shape[1]
            proj = proj.reshape(bsz, tlen, n_proj)
            conv_dim = 3 * mix
            new_conv.append(proj[:, tlen - (CONV_WIDTH - 1):, :conv_dim])
            conv_buf = jnp.pad(conv_states[l], ((0, 0), (SUBLANES - (CONV_WIDTH - 1), 0), (0, 0)))
            o, s_out = _gdn(proj, ba.reshape(bsz, tlen, LANES), conv_buf, P["conv_w_a"][l], P["a_log"][l],
                            P["dt_bias"][l], P["onorm_a"][l], delta_states[l], nc=GDN_CHUNKS if tlen % (GDN_CHUNKS * CHUNK) == 0 else 1, hg=GDN_HEADS)
            new_delta.append(s_out)
            qm_src, qm_block = proj, (conv_dim + mix) // MEM_WIDTH
        else:
            jb = l - n_a
            (proj,) = _proj(h, ng[2], [w["in_b"][jb]], tm=512, tn=2048, name="proj_b")
            proj = proj.reshape(bsz, tlen, d)
            lam0 = _lambda_init(l)
            k3 = k16.reshape(bsz, tlen, mix)
            v3 = v16.reshape(bsz, tlen, mix)
            if blocked:
                tq = _row_tile(tlen, ATTN_Q_TILE)
                tk = _row_tile(tq, ATTN_K_TILE)
                band = tq // tk + 1
                near = _bias_window(P["rel_bias"], tq, tq, tq - tk, band * tk)
                bias = jnp.stack([near[:, :, n * tk:(n + 1) * tk] for n in range(band)], axis=1) * LOG2E
                assert tk >= MAX_DISTANCE and tq % CHUNK == 0
                far = P["rel_bias"][_rel_bucket(jnp.int32(-2 * MAX_DISTANCE))] * LOG2E
                o = _diff_attention(proj, k3, v3, bias, far, P["lambda_qk"][jb], P["subln_b"][jb], lam0,
                                    n_heads=diff_heads, tq=tq, tk=tk, causal=True)
            else:
                past = past_k.shape[1]
                total = past + tlen
                padded = -(-total // LANES) * LANES
                grow = lambda c, n: jnp.pad(jnp.concatenate([c.reshape(bsz, past, mix).astype(BF16), n], axis=1),
                                            ((0, 0), (0, padded - total), (0, 0)))
                k_pos = jnp.arange(padded)
                bias = _bias_window(P["rel_bias"], q_pos0, tlen, 0, padded)
                bias = jnp.where(k_pos[None, None, :] < total, bias, NEG_INF)[:, None] * LOG2E
                o = _diff_attention(proj, grow(past_k, k3), grow(past_v, v3), bias, jnp.zeros((diff_heads,), F32),
                                    P["lambda_qk"][jb], P["subln_b"][jb], lam0, n_heads=diff_heads, tq=tlen,
                                    tk=padded, causal=False)
            qm_src, qm_block = proj, mix // MEM_WIDTH
        h = _mix_out(o, qm_src, qm_block, mem_k[l], mem_v[l], w["out"][l], h.reshape(bsz, tlen, d), ng[3])
        h = _ffn(h.reshape(rows, d), ng[4], ng[5], w["gu"], w["down"], l, 1)
    return h.reshape(bsz, tlen, d), jnp.stack(new_conv), jnp.stack(new_delta), k_new, v_new


def kernel(x_prompt, x_sample, mem_prompt, cache_k, cache_v, cache_mem_k, cache_mem_v, state_delta, state_conv,
           norm_gains, ffn_gate_up, ffn_down, w_out, mem_norm, w_mem_kv, w_in_a, conv_w_a, a_log, dt_bias,
           onorm_a, kv_norm, w_kv, w_in_b, lambda_qk, subln_b, rel_bias):
    W = _prep_weights(norm_gains, ffn_gate_up, ffn_down, w_out, w_mem_kv, w_in_a, conv_w_a, w_kv, w_in_b)
    w, n_a, n_heads, mix = W
    P = {"norm_gains": norm_gains, "kv_norm": kv_norm, "conv_w_a": conv_w_a, "a_log": a_log, "dt_bias": dt_bias,
         "onorm_a": onorm_a, "lambda_qk": lambda_qk, "subln_b": subln_b, "rel_bias": rel_bias}
    depth = norm_gains.shape[0]
    bp, tp, d = x_prompt.shape
    bs, ts, _ = x_sample.shape
    mlen = mem_prompt.shape[1]
    diff_heads = mix // (2 * HEAD_DIM)

    mem_rows = mem_prompt.reshape(bp * mlen, d)
    mem_kv = [_proj(mem_rows, mem_norm[l], [w["mem_k"][l], w["mem_v"][l]], tm=256, tn=512, name="proj_mem")
              for l in range(depth)]
    mem_k_p = jnp.stack([kv[0] for kv in mem_kv]).reshape(depth, bp, mlen, MEM_HEADS, HEAD_DIM)
    mem_v_p = jnp.stack([kv[1] for kv in mem_kv]).reshape(depth, bp, mlen, MEM_HEADS, HEAD_DIM)
    conv0 = jnp.zeros((n_a, bp, CONV_WIDTH - 1, 3 * mix), F32)
    delta0 = jnp.zeros((n_a, bp, n_heads, HEAD_DIM, HEAD_DIM), F32)
    flat = lambda m: m.reshape(m.shape[0], m.shape[1], m.shape[2], MEM_WIDTH).astype(BF16)
    y_p, conv_p, delta_p, k_p, v_p = _run_group(
        x_prompt, flat(mem_k_p), flat(mem_v_p), conv0, delta0, None, None, 0, W, P, blocked=True)

    y_s, conv_s, delta_s, k_s, v_s = _run_group(
        x_sample, flat(cache_mem_k), flat(cache_mem_v), state_conv, state_delta, cache_k, cache_v,
        cache_k.shape[1], W, P, blocked=False)

    shape_k = lambda a, b, t: a.reshape(b, t, diff_heads, 2, HEAD_DIM)
    shape_v = lambda a, b, t: a.reshape(b, t, diff_heads, 2 * HEAD_DIM)
    return (y_p, y_s, delta_p, conv_p, shape_k(k_p, bp, tp), shape_v(v_p, bp, tp), mem_k_p, mem_v_p,
            delta_s, conv_s, shape_k(k_s, bs, ts), shape_v(v_s, bs, ts))
```

```python
import functools
import math

import jax
import jax.numpy as jnp
from jax import lax
from jax.experimental import pallas as pl
from jax.experimental.pallas import tpu as pltpu

F32 = jnp.float32
BF16 = jnp.bfloat16

EPS = 1e-6
CHUNK = 64
HEAD_DIM = 128
MEM_HEADS = 4
MEM_WIDTH = MEM_HEADS * HEAD_DIM
CONV_WIDTH = 4
N_BUCKETS = 32
MAX_DISTANCE = 128
NEG_INF = -1e30
LOG2E = math.log2(math.e)

LANES = 128
SUBLANES = 8
VMEM_LIMIT = 56 * 1024 * 1024

ROW_TILE = 512
FFN_ROW_TILE = 1024
FFN_VMEM_LIMIT = 60 * 1024 * 1024
ATTN_Q_TILE = 1024
ATTN_K_TILE = 1024
GDN_CHUNKS = 4
GDN_HEADS = 4


def _params(*sem):
    return pltpu.CompilerParams(dimension_semantics=sem, vmem_limit_bytes=VMEM_LIMIT)


def _dot(a, b, precision=None):
    return jnp.dot(a, b, preferred_element_type=F32, precision=precision)


def _dot_nt(a, b):
    return lax.dot_general(a, b, (((1,), (1,)), ((), ())), preferred_element_type=F32)


def _dot_tn(a, b):
    return lax.dot_general(a, b, (((0,), (0,)), ((), ())), preferred_element_type=F32)


def _rms(x, gain):
    return x * lax.rsqrt(jnp.mean(x * x, axis=-1, keepdims=True) + EPS) * gain


def _silu(x):
    return x * jax.nn.sigmoid(x)


def _row_tile(rows, want):
    t = min(want, rows)
    assert rows % t == 0, (rows, t)
    return t


def _ffn_body(nf, x_ref, g0_ref, g1_ref, wg_ref, wu_ref, wd_ref, o_ref, xn_ref):
    f = pl.program_id(1)

    def partial_product():
        xn = xn_ref[...]
        gate = _dot(xn, wg_ref[...])
        up = _dot(xn, wu_ref[...])
        return _dot((_silu(gate) * up).astype(BF16), wd_ref[...])

    def finish(acc):
        o_ref[...] = x_ref[...] + 0.5 * _rms(acc, g1_ref[...])

    @pl.when(f == 0)
    def _():
        xn_ref[...] = _rms(x_ref[...], g0_ref[...]).astype(BF16)
        if nf == 1:
            finish(partial_product())
        else:
            o_ref[...] = partial_product()

    if nf > 1:
        @pl.when((f > 0) & (f < nf - 1))
        def _():
            o_ref[...] += partial_product()

        @pl.when(f == nf - 1)
        def _():
            finish(o_ref[...] + partial_product())


def _ffn(x, g0, g1, w_gu, w_down, layer, half, *, tf=512):
    rows, d = x.shape
    dff = w_down.shape[2]
    tm = _row_tile(rows, FFN_ROW_TILE)
    tf = _row_tile(dff, tf)
    nf = dff // tf
    return pl.pallas_call(
        functools.partial(_ffn_body, nf),
        grid=(rows // tm, nf),
        in_specs=[
            pl.BlockSpec((tm, d), lambda i, f: (i, 0), pipeline_mode=pl.Buffered(1)),
            pl.BlockSpec((1, d), lambda i, f: (0, 0)),
            pl.BlockSpec((1, d), lambda i, f: (0, 0)),
            pl.BlockSpec((None, None, d, tf), lambda i, f: (layer, half, 0, f)),
            pl.BlockSpec((None, None, d, tf), lambda i, f: (layer, half, 0, f + nf)),
            pl.BlockSpec((None, None, tf, d), lambda i, f: (layer, half, f, 0)),
        ],
        out_specs=pl.BlockSpec((tm, d), lambda i, f: (i, 0)),
        out_shape=jax.ShapeDtypeStruct((rows, d), F32),
        scratch_shapes=[pltpu.VMEM((tm, d), BF16)],
        compiler_params=pltpu.CompilerParams(dimension_semantics=("parallel", "arbitrary"),
                                             vmem_limit_bytes=FFN_VMEM_LIMIT),
        name="ffn",
    )(x, g0.reshape(1, d), g1.reshape(1, d), w_gu, w_gu, w_down)


def _proj_body(n_main, has_side, with_bf16, *refs):
    x_ref, g_ref = refs[0], refs[1]
    w_refs = refs[2:2 + n_main]
    pos = 2 + n_main
    ws_ref = refs[pos] if has_side else None
    pos += int(has_side)
    o_refs = refs[pos:pos + n_main]
    pos += n_main
    o16_refs = refs[pos:pos + n_main] if with_bf16 else [None] * n_main
    pos += n_main * int(with_bf16)
    os_ref = refs[pos] if has_side else None
    pos += int(has_side)
    xn_ref = refs[pos]

    @pl.when(pl.program_id(1) == 0)
    def _():
        xn_ref[...] = _rms(x_ref[...], g_ref[...]).astype(BF16)
        if has_side:
            os_ref[...] = _dot(xn_ref[...], ws_ref[...])

    xn = xn_ref[...]
    for w_ref, o_ref, o16_ref in zip(w_refs, o_refs, o16_refs):
        y = _dot(xn, w_ref[...])
        if len(o_ref.shape) == 2:
            o_ref[...] = y
        else:
            _, nh, nm, dh = o_ref.shape
            for hh in range(nh):
                for mm in range(nm):
                    c0 = (hh * nm + mm) * dh
                    o_ref[:, hh, mm, :] = y[:, c0:c0 + dh]
        if with_bf16:
            o16_ref[...] = y.astype(BF16)


def _proj(x, g, w_mains, w_side=None, *, tm, tn, with_bf16=False, split_first=None, name="proj"):
    rows, d = x.shape
    n = w_mains[0].shape[1]
    assert all(w.shape == (d, n) for w in w_mains)
    tm = _row_tile(rows, tm)
    tn = _row_tile(n, tn)
    has_side = w_side is not None
    in_specs = [pl.BlockSpec((tm, d), lambda i, j: (i, 0)), pl.BlockSpec((1, d), lambda i, j: (0, 0))]
    in_specs += [pl.BlockSpec((d, tn), lambda i, j: (0, j)) for _ in w_mains]
    out_specs = [pl.BlockSpec((tm, tn), lambda i, j: (i, j)) for _ in w_mains]
    out_shape = [jax.ShapeDtypeStruct((rows, n), F32) for _ in w_mains]
    if split_first is not None:
        nm = split_first
        group = nm * HEAD_DIM
        out_specs[0] = pl.BlockSpec((tm, tn // group, nm, HEAD_DIM), lambda i, j: (i, j, 0, 0))
        out_shape[0] = jax.ShapeDtypeStruct((rows, n // group, nm, HEAD_DIM), F32)
    if with_bf16:
        out_specs += [pl.BlockSpec((tm, tn), lambda i, j: (i, j)) for _ in w_mains]
        out_shape += [jax.ShapeDtypeStruct((rows, n), BF16) for _ in w_mains]
    args = [x, g.reshape(1, d), *w_mains]
    if has_side:
        ns = w_side.shape[1]
        in_specs.append(pl.BlockSpec((d, ns), lambda i, j: (0, 0)))
        out_specs.append(pl.BlockSpec((tm, ns), lambda i, j: (i, 0)))
        out_shape.append(jax.ShapeDtypeStruct((rows, ns), F32))
        args.append(w_side)
    return pl.pallas_call(
        functools.partial(_proj_body, len(w_mains), has_side, with_bf16),
        grid=(rows // tm, n // tn),
        in_specs=in_specs,
        out_specs=out_specs,
        out_shape=out_shape,
        scratch_shapes=[pltpu.VMEM((tm, d), BF16)],
        compiler_params=_params("parallel", "arbitrary"),
        name=name,
    )(*args)


def _gdn_body(nc, hg, n_heads, q_ref, k_ref, v_ref, z_ref, ba_ref, cq_ref, ck_ref, cv_ref, wq_ref, wk_ref, wv_ref,
              alog_ref, dtb_ref, onorm_ref, s0_ref, o_ref, sout_ref, xs_ref, state_ref):
    head0 = pl.program_id(1) * hg
    t = pl.program_id(2)
    tr = nc * CHUNK
    halo = SUBLANES

    @pl.when(t == 0)
    def _():
        xs_ref[0, 0:halo, :] = cq_ref[...]
        xs_ref[1, 0:halo, :] = ck_ref[...]
        xs_ref[2, 0:halo, :] = cv_ref[...]
        state_ref[...] = s0_ref[...]

    def conv(c, x_ref, w_ref):
        xs_ref[c, halo:halo + tr, :] = x_ref[...]
        w = w_ref[...]
        y = sum(xs_ref[c, halo - 3 + j:halo - 3 + j + tr, :] * w[j:j + 1, :] for j in range(CONV_WIDTH))
        xs_ref[c, 0:halo, :] = xs_ref[c, tr:tr + halo, :]
        return _silu(y)

    q_all = conv(0, q_ref, wq_ref)
    k_all = conv(1, k_ref, wk_ref)
    v_all = conv(2, v_ref, wv_ref)

    ba = ba_ref[...]
    g_all =-jnp.exp(alog_ref[...]) * jax.nn.softplus(ba + dtb_ref[...])

    row = lax.broadcasted_iota(jnp.int32, (CHUNK, 2 * CHUNK), 0)
    lane2 = lax.broadcasted_iota(jnp.int32, (CHUNK, 2 * CHUNK), 1)
    col = lane2 & (CHUNK - 1)
    left = lane2 < CHUNK
    incl = row >= col
    zeros_c = jnp.zeros((CHUNK, HEAD_DIM), F32)
    chunks = [slice(c * CHUNK, (c + 1) * CHUNK) for c in range(nc)]

    ctx = []
    for hd, sl in [(hd, sl) for hd in range(hg) for sl in chunks]:
        hs = slice(hd * HEAD_DIM, (hd + 1) * HEAD_DIM)
        qc, kc, vc = q_all[sl, hs], k_all[sl, hs], v_all[sl, hs]
        qc = qc * lax.rsqrt(jnp.sum(qc * qc, axis=-1, keepdims=True) + EPS) * (HEAD_DIM ** -0.5)
        kc = kc * lax.rsqrt(jnp.sum(kc * kc, axis=-1, keepdims=True) + EPS)
        bc = jax.nn.sigmoid(jnp.sum(jnp.where(lane2 == head0 + hd, ba[sl], 0.0), axis=1, keepdims=True))
        gc = jnp.sum(jnp.where(lane2 == head0 + hd + n_heads, g_all[sl], 0.0), axis=1, keepdims=True)
        g_row = jnp.sum(jnp.where(row == col, gc, 0.0), axis=0, keepdims=True)
        cum_col = jnp.sum(jnp.where(left, jnp.where(incl, g_row, 0.0), 0.0), axis=1, keepdims=True)
        cum_row = jnp.sum(jnp.where(row <= col, gc, 0.0), axis=0, keepdims=True)
        g_last = jnp.sum(gc, axis=0, keepdims=True)
        decay = jnp.where(incl, jnp.exp(jnp.where(incl, cum_col - cum_row, 0.0)), 0.0)
        kb = kc * bc
        kk_qk = _dot_nt(jnp.concatenate([kb, qc], axis=0).astype(BF16),
                        jnp.concatenate([kc, kc], axis=0).astype(BF16))
        lower = jnp.where(row > col, kk_qk[:CHUNK] * decay, 0.0)
        a_intra = jnp.where(incl, kk_qk[CHUNK:] * decay, 0.0)
        e_cum = jnp.exp(cum_col)
        ctx.append(dict(
            hd=hd, sl=sl, hs=hs,
            x=jnp.where(left, jnp.where(row == col, 1.0, 0.0), -lower),
            y=jnp.where(left, -lower, jnp.where(row == col, 1.0, 0.0)),
            rhs=jnp.concatenate([vc * bc, kb * e_cum], axis=1),
            a=a_intra.astype(BF16), qg=qc * e_cum,
            kd=(kc * jnp.exp(g_last - cum_col)).astype(BF16), gl=jnp.exp(g_last)))

    def split(a):
        hi16 = a.astype(BF16)
        hi = hi16.astype(F32)
        lo = a - hi
        return hi16, hi, lo.astype(BF16), lo

    zeros_rhs = jnp.zeros((CHUNK, 4 * CHUNK), BF16)
    for _ in range(6):
        for cx in ctx:
            xh16, xh, xl16, _ = split(cx["x"])
            yh16, yh, yl16, yl = split(cx["y"])
            lhs = jnp.concatenate([jnp.where(left, yl, xh), jnp.where(left, yh, 0.0)], axis=1).astype(BF16)
            top = jnp.concatenate([xh16, yh16], axis=1)
            rhs = jnp.concatenate([top, jnp.concatenate([xl16, yl16], axis=1), top, zeros_rhs], axis=0)
            prod = _dot(lhs, rhs)
            cx["x"] = prod[:, :2 * CHUNK] + jnp.where(left, cx["x"], 0.0)
            cx["y"] = prod[:, 2 * CHUNK:] + jnp.where(left, 0.0, cx["y"])

    for cx in ctx:
        t16 = jnp.where(left, cx["x"], 0.0).astype(BF16)
        uw = _dot(t16, jnp.concatenate([cx["rhs"], jnp.zeros_like(cx["rhs"])], axis=0).astype(BF16))
        cx["u"] = uw[:, :HEAD_DIM]
        cx["wq"] = jnp.concatenate([uw[:, HEAD_DIM:], cx["qg"]], axis=0).astype(BF16)

    states = [state_ref[hd] for hd in range(hg)]
    for cx in sorted(ctx, key=lambda cx: (cx["sl"].start, cx["hd"])):
        hd, sl, hs = cx["hd"], cx["sl"], cx["hs"]
        ws = _dot(cx["wq"], states[hd].astype(BF16))
        v_new = cx["u"] - ws[:CHUNK]
        o = ws[CHUNK:] + _dot(cx["a"], jnp.concatenate([v_new, zeros_c], axis=0).astype(BF16))
        states[hd] = states[hd] * cx["gl"] + _dot_tn(cx["kd"], v_new.astype(BF16))
        o_ref[sl, hs] = _rms(o, onorm_ref[...]) * _silu(z_ref[sl, hs])

    for hd in range(hg):
        state_ref[hd] = states[hd]
        sout_ref[hd] = states[hd]


def _gdn(proj, ba, conv_buf, conv_w, a_log, dt_bias, onorm, s0, *, nc, hg):
    bsz, tlen, _ = proj.shape
    n_heads = s0.shape[1]
    tr = nc * CHUNK
    assert tlen % tr == 0 and n_heads % hg == 0
    nt = tlen // tr
    ngroups = n_heads // hg
    width = hg * HEAD_DIM
    pad = lambda a: jnp.zeros((1, LANES), F32).at[0, n_heads:2 * n_heads].set(a)

    def col(c):
        return pl.BlockSpec((None, tr, width), lambda b, h, t: (b, t, c * ngroups + h))

    def buf(c):
        return pl.BlockSpec((None, SUBLANES, width), lambda b, h, t: (b, 0, c * ngroups + h))

    def tap(c):
        return pl.BlockSpec((CONV_WIDTH, width), lambda b, h, t: (0, c * ngroups + h))

    vec = pl.BlockSpec((1, LANES), lambda b, h, t: (0, 0))
    state_spec = pl.BlockSpec((None, hg, HEAD_DIM, HEAD_DIM), lambda b, h, t: (b, h, 0, 0))
    return pl.pallas_call(
        functools.partial(_gdn_body, nc, hg, n_heads),
        grid=(bsz, ngroups, nt),
        in_specs=[col(0), col(1), col(2), col(3),
                  pl.BlockSpec((None, tr, LANES), lambda b, h, t: (b, t, 0)),
                  buf(0), buf(1), buf(2), tap(0), tap(1), tap(2), vec, vec, vec, state_spec],
        out_specs=[pl.BlockSpec((None, tr, width), lambda b, h, t: (b, t, h)), state_spec],
        out_shape=[jax.ShapeDtypeStruct((bsz, tlen, n_heads * HEAD_DIM), F32),
                   jax.ShapeDtypeStruct(s0.shape, F32)],
        scratch_shapes=[pltpu.VMEM((3, tr + SUBLANES, width), F32), pltpu.VMEM((hg, HEAD_DIM, HEAD_DIM), F32)],
        compiler_params=_params("parallel", "parallel", "arbitrary"),
        name="gdn",
    )(proj, proj, proj, proj, ba, conv_buf, conv_buf, conv_buf, conv_w, conv_w, conv_w,
      pad(a_log), pad(dt_bias), onorm.reshape(1, HEAD_DIM), s0)


def _diff_body(causal, fold, ratio, lam0, it_ref, jt_ref, far_ref, q_ref, qn_ref, k_ref, v_ref, bias_ref, lq_ref,
               sub_ref, o_ref, qs_ref, sa_ref, sb_ref, m_ref, l_ref, acc_ref):
    step = pl.program_id(2)
    i = it_ref[step]
    j = jt_ref[step]
    last = (i + 1) * ratio - 1 if causal else pl.num_programs(2) - 2
    first_near = i * ratio - 1 if causal else 0
    first_j = jnp.where(i == 0, 0, 1) if fold else 0
    tk = k_ref.shape[0]
    dk = HEAD_DIM
    tq = q_ref.shape[0]
    rb = 256 if tq % 256 == 0 else tq
    units = [(mp, slice(r * rb, (r + 1) * rb)) for r in range(tq // rb) for mp in range(2)]

    def load_queries(src_ref):
        q = src_ref[...] * (dk ** -0.5 * LOG2E)
        qs_ref[0] = q[:, :dk].astype(BF16)
        qs_ref[1] = q[:, dk:].astype(BF16)

    @pl.when(j == 0)
    def _():
        load_queries(q_ref)

    @pl.when(j == first_j)
    def _():
        m_ref[...] = jnp.full(m_ref.shape, NEG_INF, F32)
        l_ref[...] = jnp.zeros(l_ref.shape, F32)
        acc_ref[...] = jnp.zeros(acc_ref.shape, F32)

    s_bufs = (sa_ref, sb_ref)

    def scores(slot, mp, rows):
        s_bufs[slot][mp, rows, :] = _dot_nt(qs_ref[mp, rows, :], k_ref[:, mp * dk:(mp + 1) * dk])

    def softmax(slot, mp, rows, bias, shift):
        s = s_bufs[slot][mp, rows, :]
        if bias is not None:
            s = s + bias[rows]
        m_prev = m_ref[mp, rows, :]
        m_next = jnp.maximum(m_prev, jnp.max(s, axis=1, keepdims=True) + shift)
        p = jnp.exp2(s - jnp.concatenate([m_next - shift] * (tk // LANES), axis=1))
        alpha = jnp.exp2(m_prev - m_next)
        l_ref[mp, rows, :] = alpha * l_ref[mp, rows, :] + jnp.sum(p, axis=1, keepdims=True)
        m_ref[mp, rows, :] = m_next
        return p.astype(BF16), alpha

    def values(mp, rows, p16, alpha):
        acc_ref[mp, rows, :] = (acc_ref[mp, rows, :] * jnp.concatenate([alpha] * (2 * dk // LANES), axis=1)
                                + _dot(p16, v_ref[...]))

    def step_body(cur, prev, bias, shift):
        pending = None
        for mp, rows in units:
            probs = softmax(prev, mp, rows, bias, shift) if prev is not None else None
            if cur is not None:
                scores(cur, mp, rows)
            if pending is not None:
                values(*pending)
            pending = (mp, rows, *probs) if probs is not None else None
        if pending is not None:
            values(*pending)

    @pl.when(j == 0)
    def _():
        step_body(0, None, None, 0.0)

    for cur in range(2):
        prev = 1 - cur
        par = (step % 2) == cur
        if causal:
            @pl.when(par & (j >= 1) & (j - 1 < first_near))
            def _():
                step_body(cur, prev, None, far_ref[pl.program_id(1)])

        @pl.when(par & (j >= 1) & (j - 1 >= first_near) & (j <= last))
        def _():
            step_body(cur, prev, bias_ref[...], 0.0)

        @pl.when(par & (j == last + 1))
        def _():
            if fold:
                load_queries(qn_ref)
                step_body(cur, prev, bias_ref[...], 0.0)
            else:
                step_body(None, prev, bias_ref[...], 0.0)

    @pl.when(j == last + 1)
    def _():
        lq = lq_ref[...]
        lam = (jnp.exp(jnp.sum(lq[0:1] * lq[1:2], axis=1, keepdims=True))
               - jnp.exp(jnp.sum(lq[2:3] * lq[3:4], axis=1, keepdims=True)) + lam0)
        reps = 2 * dk // LANES
        o0 = acc_ref[0] / jnp.concatenate([l_ref[0]] * reps, axis=1)
        o1 = acc_ref[1] / jnp.concatenate([l_ref[1]] * reps, axis=1)
        o_ref[...] = _rms(o0 - lam * o1, sub_ref[...]) * (1.0 - lam0)


def _diff_attention(q, k, v, bias, far, lam_qk, subln, lam0, *, n_heads, tq, tk, causal):
    bsz, tq_len, _ = q.shape
    tk_len = k.shape[1]
    nq, nk = tq_len // tq, tk_len // tk
    dv = 2 * HEAD_DIM
    assert tq % tk == 0 if causal else nq == 1
    ratio = tq // tk if causal else 1

    def last_tile(i):
        return (i + 1) * ratio - 1 if causal else nk - 1

    fold = causal and nq > 1
    pairs = [(i, j) for i in range(nq) for j in range(1 if (fold and i > 0) else 0, last_tile(i) + 2)]
    i_tab = jnp.asarray([p[0] for p in pairs], jnp.int32)
    j_tab = jnp.asarray([p[1] for p in pairs], jnp.int32)

    def q_map(b, h, s, it, jt):
        return (b, it[s], h)

    def qn_map(b, h, s, it, jt):
        return (b, jnp.minimum(it[s] + 1, nq - 1), h)

    def k_map(b, h, s, it, jt):
        tile = jnp.minimum(jt[s], last_tile(it[s]))
        if fold:
            tile = jnp.where(jt[s] == last_tile(it[s]) + 1, 0, tile)
        return (b, tile, h)

    def v_map(b, h, s, it, jt):
        return (b, jnp.clip(jt[s] - 1, 0, last_tile(it[s])), h)

    def bias_map(b, h, s, it, jt):
        if causal:
            return (h, jnp.clip(jt[s] - it[s] * ratio, 0, ratio), 0, 0)
        return (h, jnp.clip(jt[s] - 1, 0, nk - 1), 0, 0)

    const2 = lambda b, h, s, it, jt: (0, 0)
    return pl.pallas_call(
        functools.partial(_diff_body, causal, fold, ratio, lam0),
        grid_spec=pltpu.PrefetchScalarGridSpec(
            num_scalar_prefetch=2,
            grid=(bsz, n_heads, len(pairs)),
            in_specs=[
                pl.BlockSpec(memory_space=pltpu.SMEM),
                pl.BlockSpec((None, tq, dv), q_map),
                pl.BlockSpec((None, tq, dv), qn_map),
                pl.BlockSpec((None, tk, dv), k_map),
                pl.BlockSpec((None, tk, dv), v_map),
                pl.BlockSpec((None, None, tq, tk), bias_map),
                pl.BlockSpec((4, HEAD_DIM), const2),
                pl.BlockSpec((1, dv), const2),
            ],
            out_specs=pl.BlockSpec((None, tq, dv), q_map),
            scratch_shapes=[pltpu.VMEM((2, tq, HEAD_DIM), BF16),
                            pltpu.VMEM((2, tq, tk), F32), pltpu.VMEM((2, tq, tk), F32),
                            pltpu.VMEM((2, tq, LANES), F32), pltpu.VMEM((2, tq, LANES), F32),
                            pltpu.VMEM((2, tq, dv), F32)],
        ),
        out_shape=jax.ShapeDtypeStruct((bsz, tq_len, n_heads * dv), F32),
        compiler_params=_params("parallel", "parallel", "arbitrary"),
        name="diff_attn",
    )(i_tab, j_tab, far, q, q, k, v, bias, lam_qk, subln.reshape(1, dv))


def _rel_bucket(rel):
    nb = N_BUCKETS // 2
    max_exact = nb // 2
    n = jnp.abs(rel)
    large = max_exact + (jnp.log(jnp.maximum(n, 1).astype(F32) / max_exact)
                         / math.log(MAX_DISTANCE / max_exact) * (nb - max_exact)).astype(jnp.int32)
    large = jnp.minimum(large, nb - 1)
    return jnp.where(rel > 0, nb, 0) + jnp.where(n < max_exact, n, large)


def _bias_window(rel_table, q0, nq, k0, nk):
    blk = min(LANES, nq)
    assert nq % blk == 0 and nk % blk == 0 and blk % CHUNK == 0
    na = nq // blk
    q_pos = q0 + jnp.arange(blk)[:, None]
    k_pos = k0 - (na - 1) * blk + jnp.arange(nk + (na - 1) * blk)[None, :]
    bucket = _rel_bucket(k_pos - q_pos)[None]
    bias = sum(jnp.where(bucket == b, rel_table[b][:, None, None], 0.0) for b in range(N_BUCKETS))
    wide = jnp.where(((k_pos // CHUNK) <= (q_pos // CHUNK))[None], bias, NEG_INF).astype(F32)
    return jnp.concatenate([wide[:, :, (na - 1 - a) * blk:(na - 1 - a) * blk + nk] for a in range(na)], axis=1)


def _mix_out_body(o_ref, qm_ref, mk_ref, mv_ref, w_ref, h_ref, g_ref, out_ref, mix_ref):
    n_o = o_ref.shape[1]
    mix_ref[:, 0:n_o] = o_ref[...].astype(BF16)
    for hd in range(MEM_HEADS):
        sl = slice(hd * HEAD_DIM, (hd + 1) * HEAD_DIM)
        s = _dot_nt(qm_ref[:, sl].astype(BF16), mk_ref[:, sl]) * (HEAD_DIM ** -0.5)
        e = jnp.exp(s - jnp.max(s, axis=1, keepdims=True))
        p = e / jnp.sum(e, axis=1, keepdims=True)
        mix_ref[:, n_o + hd * HEAD_DIM:n_o + (hd + 1) * HEAD_DIM] = _dot(p.astype(BF16), mv_ref[:, sl]).astype(BF16)
    y = _dot(mix_ref[...], w_ref[...])
    out_ref[...] = h_ref[...] + _rms(y, g_ref[...])


def _mix_out(o, qm_src, qm_block, mem_k, mem_v, w_out, h, g):
    bsz, tlen, n_o = o.shape
    d = h.shape[2]
    mlen = mem_k.shape[1]
    tm = _row_tile(tlen, ROW_TILE)
    return pl.pallas_call(
        _mix_out_body,
        grid=(bsz, tlen // tm),
        in_specs=[
            pl.BlockSpec((None, tm, n_o), lambda b, i: (b, i, 0)),
            pl.BlockSpec((None, tm, MEM_WIDTH), lambda b, i: (b, i, qm_block)),
            pl.BlockSpec((None, mlen, MEM_WIDTH), lambda b, i: (b, 0, 0)),
            pl.BlockSpec((None, mlen, MEM_WIDTH), lambda b, i: (b, 0, 0)),
            pl.BlockSpec((d, d), lambda b, i: (0, 0)),
            pl.BlockSpec((None, tm, d), lambda b, i: (b, i, 0)),
            pl.BlockSpec((1, d), lambda b, i: (0, 0)),
        ],
        out_specs=pl.BlockSpec((None, tm, d), lambda b, i: (b, i, 0)),
        out_shape=jax.ShapeDtypeStruct(h.shape, F32),
        scratch_shapes=[pltpu.VMEM((tm, d), BF16)],
        compiler_params=_params("parallel", "parallel"),
        name="mix_out",
    )(o, qm_src, mem_k, mem_v, w_out, h, g.reshape(1, d))


def _lambda_init(layer_idx):
    return 0.8 - 0.6 * math.exp(-0.3 * layer_idx)


def _prep_weights(norm_gains, ffn_gate_up, ffn_down, w_out, w_mem_kv, w_in_a, conv_w_a, w_kv, w_in_b):
    d = w_out.shape[1]
    n_a = w_in_a.shape[0]
    conv_dim = conv_w_a.shape[2]
    mix = d - MEM_WIDTH
    n_heads = mix // HEAD_DIM
    qkvz = conv_dim + mix
    w = {
        "gu": ffn_gate_up.astype(BF16), "down": ffn_down.astype(BF16), "out": w_out.astype(BF16),
        "mem_k": w_mem_kv[:, :, :MEM_WIDTH].astype(BF16), "mem_v": w_mem_kv[:, :, MEM_WIDTH:].astype(BF16),
        "in_a": jnp.concatenate([w_in_a[:, :, :qkvz], w_in_a[:, :, qkvz + 2 * n_heads:]], axis=2).astype(BF16),
        "in_a_ba": jnp.pad(w_in_a[:, :, qkvz:qkvz + 2 * n_heads],
                           ((0, 0), (0, 0), (0, LANES - 2 * n_heads))).astype(BF16),
        "kv_k": w_kv[:, :mix].astype(BF16), "kv_v": w_kv[:, mix:].astype(BF16),
        "in_b": w_in_b.astype(BF16),
    }
    return w, n_a, n_heads, mix


def _run_group(x, mem_k, mem_v, conv_states, delta_states, past_k, past_v, q_pos0, W, P, *, blocked):
    w, n_a, n_heads, mix = W
    bsz, tlen, d = x.shape
    rows = bsz * tlen
    depth = P["norm_gains"].shape[0]
    diff_heads = mix // (2 * HEAD_DIM)
    h = x.reshape(rows, d)
    new_conv, new_delta = [], []
    k_new = v_new = None
    for l in range(depth):
        ng = P["norm_gains"][l]
        if l == n_a:
            k_new, v_new, k16, v16 = _proj(h, P["kv_norm"], [w["kv_k"], w["kv_v"]], tm=512, tn=768,
                                           with_bf16=True, split_first=2, name="proj_kv")
        h = _ffn(h, ng[0], ng[1], w["gu"], w["down"], l, 0)
        if l < n_a:
            proj, ba = _proj(h, ng[2], [w["in_a"][l]], w["in_a_ba"][l], tm=1024, tn=1664, name="proj_a")
            n_proj = proj.shape[1]
            proj = proj.reshape(bsz, tlen, n_proj)
            conv_dim = 3 * mix
            new_conv.append(proj[:, tlen - (CONV_WIDTH - 1):, :conv_dim])
            conv_buf = jnp.pad(conv_states[l], ((0, 0), (SUBLANES - (CONV_WIDTH - 1), 0), (0, 0)))
            o, s_out = _gdn(proj, ba.reshape(bsz, tlen, LANES), conv_buf, P["conv_w_a"][l], P["a_log"][l],
                            P["dt_bias"][l], P["onorm_a"][l], delta_states[l], nc=GDN_CHUNKS if tlen % (GDN_CHUNKS * CHUNK) == 0 else 1, hg=GDN_HEADS)
            new_delta.append(s_out)
            qm_src, qm_block = proj, (conv_dim + mix) // MEM_WIDTH
        else:
            jb = l - n_a
            (proj,) = _proj(h, ng[2], [w["in_b"][jb]], tm=512, tn=2048, name="proj_b")
            proj = proj.reshape(bsz, tlen, d)
            lam0 = _lambda_init(l)
            k3 = k16.reshape(bsz, tlen, mix)
            v3 = v16.reshape(bsz, tlen, mix)
            if blocked:
                tq = _row_tile(tlen, ATTN_Q_TILE)
                tk = _row_tile(tq, ATTN_K_TILE)
                band = tq // tk + 1
                near = _bias_window(P["rel_bias"], tq, tq, tq - tk, band * tk)
                bias = jnp.stack([near[:, :, n * tk:(n + 1) * tk] for n in range(band)], axis=1) * LOG2E
                assert tk >= MAX_DISTANCE and tq % CHUNK == 0
                far = P["rel_bias"][_rel_bucket(jnp.int32(-2 * MAX_DISTANCE))] * LOG2E
                o = _diff_attention(proj, k3, v3, bias, far, P["lambda_qk"][jb], P["subln_b"][jb], lam0,
                                    n_heads=diff_heads, tq=tq, tk=tk, causal=True)
            else:
                past = past_k.shape[1]
                total = past + tlen
                padded = -(-total // LANES) * LANES
                grow = lambda c, n: jnp.pad(jnp.concatenate([c.reshape(bsz, past, mix).astype(BF16), n], axis=1),
                                            ((0, 0), (0, padded - total), (0, 0)))
                k_pos = jnp.arange(padded)
                bias = _bias_window(P["rel_bias"], q_pos0, tlen, 0, padded)
                bias = jnp.where(k_pos[None, None, :] < total, bias, NEG_INF)[:, None] * LOG2E
                o = _diff_attention(proj, grow(past_k, k3), grow(past_v, v3), bias, jnp.zeros((diff_heads,), F32),
                                    P["lambda_qk"][jb], P["subln_b"][jb], lam0, n_heads=diff_heads, tq=tlen,
                                    tk=padded, causal=False)
            qm_src, qm_block = proj, mix // MEM_WIDTH
        h = _mix_out(o, qm_src, qm_block, mem_k[l], mem_v[l], w["out"][l], h.reshape(bsz, tlen, d), ng[3])
        h = _ffn(h.reshape(rows, d), ng[4], ng[5], w["gu"], w["down"], l, 1)
    return h.reshape(bsz, tlen, d), jnp.stack(new_conv), jnp.stack(new_delta), k_new, v_new


def kernel(x_prompt, x_sample, mem_prompt, cache_k, cache_v, cache_mem_k, cache_mem_v, state_delta, state_conv,
           norm_gains, ffn_gate_up, ffn_down, w_out, mem_norm, w_mem_kv, w_in_a, conv_w_a, a_log, dt_bias,
           onorm_a, kv_norm, w_kv, w_in_b, lambda_qk, subln_b, rel_bias):
    W = _prep_weights(norm_gains, ffn_gate_up, ffn_down, w_out, w_mem_kv, w_in_a, conv_w_a, w_kv, w_in_b)
    w, n_a, n_heads, mix = W
    P = {"norm_gains": norm_gains, "kv_norm": kv_norm, "conv_w_a": conv_w_a, "a_log": a_log, "dt_bias": dt_bias,
         "onorm_a": onorm_a, "lambda_qk": lambda_qk, "subln_b": subln_b, "rel_bias": rel_bias}
    depth = norm_gains.shape[0]
    bp, tp, d = x_prompt.shape
    bs, ts, _ = x_sample.shape
    mlen = mem_prompt.shape[1]
    diff_heads = mix // (2 * HEAD_DIM)

    mem_rows = mem_prompt.reshape(bp * mlen, d)
    mem_kv = [_proj(mem_rows, mem_norm[l], [w["mem_k"][l], w["mem_v"][l]], tm=256, tn=512, name="proj_mem")
              for l in range(depth)]
    mem_k_p = jnp.stack([kv[0] for kv in mem_kv]).reshape(depth, bp, mlen, MEM_HEADS, HEAD_DIM)
    mem_v_p = jnp.stack([kv[1] for kv in mem_kv]).reshape(depth, bp, mlen, MEM_HEADS, HEAD_DIM)
    conv0 = jnp.zeros((n_a, bp, CONV_WIDTH - 1, 3 * mix), F32)
    delta0 = jnp.zeros((n_a, bp, n_heads, HEAD_DIM, HEAD_DIM), F32)
    flat = lambda m: m.reshape(m.shape[0], m.shape[1], m.shape[2], MEM_WIDTH).astype(BF16)
    y_p, conv_p, delta_p, k_p, v_p = _run_group(
        x_prompt, flat(mem_k_p), flat(mem_v_p), conv0, delta0, None, None, 0, W, P, blocked=True)

    y_s, conv_s, delta_s, k_s, v_s = _run_group(
        x_sample, flat(cache_mem_k), flat(cache_mem_v), state_conv, state_delta, cache_k, cache_v,
        cache_k.shape[1], W, P, blocked=False)

    shape_k = lambda a, b, t: a.reshape(b, t, diff_heads, 2, HEAD_DIM)
    shape_v = lambda a, b, t: a.reshape(b, t, diff_heads, 2 * HEAD_DIM)
    return (y_p, y_s, delta_p, conv_p, shape_k(k_p, bp, tp), shape_v(v_p, bp, tp), mem_k_p, mem_v_p,
            delta_s, conv_s, shape_k(k_s, bs, ts), shape_v(v_s, bs, ts))
```

```python
import functools
import math

import jax
import jax.numpy as jnp
from jax import lax
from jax.experimental import pallas as pl
from jax.experimental.pallas import tpu as pltpu

F32 = jnp.float32
BF16 = jnp.bfloat16

EPS = 1e-6
CHUNK = 64
HEAD_DIM = 128
MEM_HEADS = 4
MEM_WIDTH = MEM_HEADS * HEAD_DIM
CONV_WIDTH = 4
N_BUCKETS = 32
MAX_DISTANCE = 128
NEG_INF = -1e30
LOG2E = math.log2(math.e)

LANES = 128
SUBLANES = 8
VMEM_LIMIT = 56 * 1024 * 1024

ROW_TILE = 512
FFN_ROW_TILE = 1024
FFN_VMEM_LIMIT = 63 * 1024 * 1024
ATTN_Q_TILE = 1024
ATTN_K_TILE = 1024
GDN_CHUNKS = 4
GDN_HEADS = 4


def _params(*sem):
    return pltpu.CompilerParams(dimension_semantics=sem, vmem_limit_bytes=VMEM_LIMIT)


def _dot(a, b, precision=None):
    return jnp.dot(a, b, preferred_element_type=F32, precision=precision)


def _dot_nt(a, b):
    return lax.dot_general(a, b, (((1,), (1,)), ((), ())), preferred_element_type=F32)


def _dot_tn(a, b):
    return lax.dot_general(a, b, (((0,), (0,)), ((), ())), preferred_element_type=F32)


def _rms(x, gain):
    return x * lax.rsqrt(jnp.mean(x * x, axis=-1, keepdims=True) + EPS) * gain


def _silu(x):
    return x * jax.nn.sigmoid(x)


def _row_tile(rows, want):
    t = min(want, rows)
    assert rows % t == 0, (rows, t)
    return t


def _ffn_body(nf, x_ref, g0_ref, g1_ref, wg_ref, wu_ref, wd_ref, o_ref, xn_ref):
    f = pl.program_id(1)

    def partial_product():
        xn = xn_ref[...]
        gate = _dot(xn, wg_ref[...])
        up = _dot(xn, wu_ref[...])
        return _dot((_silu(gate) * up).astype(BF16), wd_ref[...])

    def finish(acc):
        o_ref[...] = x_ref[...] + 0.5 * _rms(acc, g1_ref[...])

    @pl.when(f == 0)
    def _():
        xn_ref[...] = _rms(x_ref[...], g0_ref[...]).astype(BF16)
        if nf == 1:
            finish(partial_product())
        else:
            o_ref[...] = partial_product()

    if nf > 1:
        @pl.when((f > 0) & (f < nf - 1))
        def _():
            o_ref[...] += partial_product()

        @pl.when(f == nf - 1)
        def _():
            finish(o_ref[...] + partial_product())


def _ffn(x, g0, g1, w_gu, w_down, layer, half, *, tf=512):
    rows, d = x.shape
    dff = w_down.shape[2]
    tm = _row_tile(rows, FFN_ROW_TILE)
    tf = _row_tile(dff, tf)
    nf = dff // tf
    return pl.pallas_call(
        functools.partial(_ffn_body, nf),
        grid=(rows // tm, nf),
        in_specs=[
            pl.BlockSpec((tm, d), lambda i, f: (i, 0)),
            pl.BlockSpec((1, d), lambda i, f: (0, 0)),
            pl.BlockSpec((1, d), lambda i, f: (0, 0)),
            pl.BlockSpec((None, None, d, tf), lambda i, f: (layer, half, 0, f)),
            pl.BlockSpec((None, None, d, tf), lambda i, f: (layer, half, 0, f + nf)),
            pl.BlockSpec((None, None, tf, d), lambda i, f: (layer, half, f, 0)),
        ],
        out_specs=pl.BlockSpec((tm, d), lambda i, f: (i, 0)),
        out_shape=jax.ShapeDtypeStruct((rows, d), F32),
        scratch_shapes=[pltpu.VMEM((tm, d), BF16)],
        compiler_params=pltpu.CompilerParams(dimension_semantics=("parallel", "arbitrary"),
                                             vmem_limit_bytes=FFN_VMEM_LIMIT),
        name="ffn",
    )(x, g0.reshape(1, d), g1.reshape(1, d), w_gu, w_gu, w_down)


def _proj_body(n_main, has_side, with_bf16, *refs):
    x_ref, g_ref = refs[0], refs[1]
    w_refs = refs[2:2 + n_main]
    pos = 2 + n_main
    ws_ref = refs[pos] if has_side else None
    pos += int(has_side)
    o_refs = refs[pos:pos + n_main]
    pos += n_main
    o16_refs = refs[pos:pos + n_main] if with_bf16 else [None] * n_main
    pos += n_main * int(with_bf16)
    os_ref = refs[pos] if has_side else None
    pos += int(has_side)
    xn_ref = refs[pos]

    @pl.when(pl.program_id(1) == 0)
    def _():
        xn_ref[...] = _rms(x_ref[...], g_ref[...]).astype(BF16)
        if has_side:
            os_ref[...] = _dot(xn_ref[...], ws_ref[...])

    xn = xn_ref[...]
    for w_ref, o_ref, o16_ref in zip(w_refs, o_refs, o16_refs):
        y = _dot(xn, w_ref[...])
        if len(o_ref.shape) == 2:
            o_ref[...] = y
        else:
            _, nh, nm, dh = o_ref.shape
            for hh in range(nh):
                for mm in range(nm):
                    c0 = (hh * nm + mm) * dh
                    o_ref[:, hh, mm, :] = y[:, c0:c0 + dh]
        if with_bf16:
            o16_ref[...] = y.astype(BF16)


def _proj(x, g, w_mains, w_side=None, *, tm, tn, with_bf16=False, split_first=None, name="proj"):
    rows, d = x.shape
    n = w_mains[0].shape[1]
    assert all(w.shape == (d, n) for w in w_mains)
    tm = _row_tile(rows, tm)
    tn = _row_tile(n, tn)
    has_side = w_side is not None
    in_specs = [pl.BlockSpec((tm, d), lambda i, j: (i, 0)), pl.BlockSpec((1, d), lambda i, j: (0, 0))]
    in_specs += [pl.BlockSpec((d, tn), lambda i, j: (0, j)) for _ in w_mains]
    out_specs = [pl.BlockSpec((tm, tn), lambda i, j: (i, j)) for _ in w_mains]
    out_shape = [jax.ShapeDtypeStruct((rows, n), F32) for _ in w_mains]
    if split_first is not None:
        nm = split_first
        group = nm * HEAD_DIM
        out_specs[0] = pl.BlockSpec((tm, tn // group, nm, HEAD_DIM), lambda i, j: (i, j, 0, 0))
        out_shape[0] = jax.ShapeDtypeStruct((rows, n // group, nm, HEAD_DIM), F32)
    if with_bf16:
        out_specs += [pl.BlockSpec((tm, tn), lambda i, j: (i, j)) for _ in w_mains]
        out_shape += [jax.ShapeDtypeStruct((rows, n), BF16) for _ in w_mains]
    args = [x, g.reshape(1, d), *w_mains]
    if has_side:
        ns = w_side.shape[1]
        in_specs.append(pl.BlockSpec((d, ns), lambda i, j: (0, 0)))
        out_specs.append(pl.BlockSpec((tm, ns), lambda i, j: (i, 0)))
        out_shape.append(jax.ShapeDtypeStruct((rows, ns), F32))
        args.append(w_side)
    return pl.pallas_call(
        functools.partial(_proj_body, len(w_mains), has_side, with_bf16),
        grid=(rows // tm, n // tn),
        in_specs=in_specs,
        out_specs=out_specs,
        out_shape=out_shape,
        scratch_shapes=[pltpu.VMEM((tm, d), BF16)],
        compiler_params=_params("parallel", "arbitrary"),
        name=name,
    )(*args)


def _gdn_body(nc, hg, n_heads, q_ref, k_ref, v_ref, z_ref, ba_ref, cq_ref, ck_ref, cv_ref, wq_ref, wk_ref, wv_ref,
              alog_ref, dtb_ref, onorm_ref, s0_ref, o_ref, sout_ref, xs_ref, state_ref):
    head0 = pl.program_id(1) * hg
    t = pl.program_id(2)
    tr = nc * CHUNK
    halo = SUBLANES

    @pl.when(t == 0)
    def _():
        xs_ref[0, 0:halo, :] = cq_ref[...]
        xs_ref[1, 0:halo, :] = ck_ref[...]
        xs_ref[2, 0:halo, :] = cv_ref[...]
        state_ref[...] = s0_ref[...]

    def conv(c, x_ref, w_ref):
        xs_ref[c, halo:halo + tr, :] = x_ref[...]
        w = w_ref[...]
        y = sum(xs_ref[c, halo - 3 + j:halo - 3 + j + tr, :] * w[j:j + 1, :] for j in range(CONV_WIDTH))
        xs_ref[c, 0:halo, :] = xs_ref[c, tr:tr + halo, :]
        return _silu(y)

    q_all = conv(0, q_ref, wq_ref)
    k_all = conv(1, k_ref, wk_ref)
    v_all = conv(2, v_ref, wv_ref)

    ba = ba_ref[...]
    g_all =-jnp.exp(alog_ref[...]) * jax.nn.softplus(ba + dtb_ref[...])

    row = lax.broadcasted_iota(jnp.int32, (CHUNK, 2 * CHUNK), 0)
    lane2 = lax.broadcasted_iota(jnp.int32, (CHUNK, 2 * CHUNK), 1)
    col = lane2 & (CHUNK - 1)
    left = lane2 < CHUNK
    incl = row >= col
    zeros_c = jnp.zeros((CHUNK, HEAD_DIM), F32)
    chunks = [slice(c * CHUNK, (c + 1) * CHUNK) for c in range(nc)]

    ctx = []
    for hd, sl in [(hd, sl) for hd in range(hg) for sl in chunks]:
        hs = slice(hd * HEAD_DIM, (hd + 1) * HEAD_DIM)
        qc, kc, vc = q_all[sl, hs], k_all[sl, hs], v_all[sl, hs]
        qc = qc * lax.rsqrt(jnp.sum(qc * qc, axis=-1, keepdims=True) + EPS) * (HEAD_DIM ** -0.5)
        kc = kc * lax.rsqrt(jnp.sum(kc * kc, axis=-1, keepdims=True) + EPS)
        bc = jax.nn.sigmoid(jnp.sum(jnp.where(lane2 == head0 + hd, ba[sl], 0.0), axis=1, keepdims=True))
        gc = jnp.sum(jnp.where(lane2 == head0 + hd + n_heads, g_all[sl], 0.0), axis=1, keepdims=True)
        g_row = jnp.sum(jnp.where(row == col, gc, 0.0), axis=0, keepdims=True)
        cum_col = jnp.sum(jnp.where(left, jnp.where(incl, g_row, 0.0), 0.0), axis=1, keepdims=True)
        cum_row = jnp.sum(jnp.where(row <= col, gc, 0.0), axis=0, keepdims=True)
        g_last = jnp.sum(gc, axis=0, keepdims=True)
        decay = jnp.where(incl, jnp.exp(jnp.where(incl, cum_col - cum_row, 0.0)), 0.0)
        kb = kc * bc
        kk_qk = _dot_nt(jnp.concatenate([kb, qc], axis=0).astype(BF16),
                        jnp.concatenate([kc, kc], axis=0).astype(BF16))
        lower = jnp.where(row > col, kk_qk[:CHUNK] * decay, 0.0)
        a_intra = jnp.where(incl, kk_qk[CHUNK:] * decay, 0.0)
        e_cum = jnp.exp(cum_col)
        ctx.append(dict(
            hd=hd, sl=sl, hs=hs,
            x=jnp.where(left, jnp.where(row == col, 1.0, 0.0), -lower),
            y=jnp.where(left, -lower, jnp.where(row == col, 1.0, 0.0)),
            rhs=jnp.concatenate([vc * bc, kb * e_cum], axis=1),
            a=a_intra.astype(BF16), qg=qc * e_cum,
            kd=(kc * jnp.exp(g_last - cum_col)).astype(BF16), gl=jnp.exp(g_last)))

    def split(a):
        hi16 = a.astype(BF16)
        hi = hi16.astype(F32)
        lo = a - hi
        return hi16, hi, lo.astype(BF16), lo

    zeros_rhs = jnp.zeros((CHUNK, 4 * CHUNK), BF16)
    for _ in range(6):
        for cx in ctx:
            xh16, xh, xl16, _ = split(cx["x"])
            yh16, yh, yl16, yl = split(cx["y"])
            lhs = jnp.concatenate([jnp.where(left, yl, xh), jnp.where(left, yh, 0.0)], axis=1).astype(BF16)
            top = jnp.concatenate([xh16, yh16], axis=1)
            rhs = jnp.concatenate([top, jnp.concatenate([xl16, yl16], axis=1), top, zeros_rhs], axis=0)
            prod = _dot(lhs, rhs)
            cx["x"] = prod[:, :2 * CHUNK] + jnp.where(left, cx["x"], 0.0)
            cx["y"] = prod[:, 2 * CHUNK:] + jnp.where(left, 0.0, cx["y"])

    for cx in ctx:
        t16 = jnp.where(left, cx["x"], 0.0).astype(BF16)
        uw = _dot(t16, jnp.concatenate([cx["rhs"], jnp.zeros_like(cx["rhs"])], axis=0).astype(BF16))
        cx["u"] = uw[:, :HEAD_DIM]
        cx["wq"] = jnp.concatenate([uw[:, HEAD_DIM:], cx["qg"]], axis=0).astype(BF16)

    states = [state_ref[hd] for hd in range(hg)]
    for cx in sorted(ctx, key=lambda cx: (cx["sl"].start, cx["hd"])):
        hd, sl, hs = cx["hd"], cx["sl"], cx["hs"]
        ws = _dot(cx["wq"], states[hd].astype(BF16))
        v_new = cx["u"] - ws[:CHUNK]
        o = ws[CHUNK:] + _dot(cx["a"], jnp.concatenate([v_new, zeros_c], axis=0).astype(BF16))
        states[hd] = states[hd] * cx["gl"] + _dot_tn(cx["kd"], v_new.astype(BF16))
        o_ref[sl, hs] = _rms(o, onorm_ref[...]) * _silu(z_ref[sl, hs])

    for hd in range(hg):
        state_ref[hd] = states[hd]
        sout_ref[hd] = states[hd]


def _gdn(proj, ba, conv_buf, conv_w, a_log, dt_bias, onorm, s0, *, nc, hg):
    bsz, tlen, _ = proj.shape
    n_heads = s0.shape[1]
    tr = nc * CHUNK
    assert tlen % tr == 0 and n_heads % hg == 0
    nt = tlen // tr
    ngroups = n_heads // hg
    width = hg * HEAD_DIM
    pad = lambda a: jnp.zeros((1, LANES), F32).at[0, n_heads:2 * n_heads].set(a)

    def col(c):
        return pl.BlockSpec((None, tr, width), lambda b, h, t: (b, t, c * ngroups + h))

    def buf(c):
        return pl.BlockSpec((None, SUBLANES, width), lambda b, h, t: (b, 0, c * ngroups + h))

    def tap(c):
        return pl.BlockSpec((CONV_WIDTH, width), lambda b, h, t: (0, c * ngroups + h))

    vec = pl.BlockSpec((1, LANES), lambda b, h, t: (0, 0))
    state_spec = pl.BlockSpec((None, hg, HEAD_DIM, HEAD_DIM), lambda b, h, t: (b, h, 0, 0))
    return pl.pallas_call(
        functools.partial(_gdn_body, nc, hg, n_heads),
        grid=(bsz, ngroups, nt),
        in_specs=[col(0), col(1), col(2), col(3),
                  pl.BlockSpec((None, tr, LANES), lambda b, h, t: (b, t, 0)),
                  buf(0), buf(1), buf(2), tap(0), tap(1), tap(2), vec, vec, vec, state_spec],
        out_specs=[pl.BlockSpec((None, tr, width), lambda b, h, t: (b, t, h)), state_spec],
        out_shape=[jax.ShapeDtypeStruct((bsz, tlen, n_heads * HEAD_DIM), F32),
                   jax.ShapeDtypeStruct(s0.shape, F32)],
        scratch_shapes=[pltpu.VMEM((3, tr + SUBLANES, width), F32), pltpu.VMEM((hg, HEAD_DIM, HEAD_DIM), F32)],
        compiler_params=_params("parallel", "parallel", "arbitrary"),
        name="gdn",
    )(proj, proj, proj, proj, ba, conv_buf, conv_buf, conv_buf, conv_w, conv_w, conv_w,
      pad(a_log), pad(dt_bias), onorm.reshape(1, HEAD_DIM), s0)


def _diff_body(causal, ratio, lam0, it_ref, jt_ref, far_ref, q_ref, k_ref, v_ref, bias_ref, lq_ref, sub_ref,
               o_ref, qs_ref, sa_ref, sb_ref, m_ref, l_ref, acc_ref):
    step = pl.program_id(2)
    i = it_ref[step]
    j = jt_ref[step]
    last = (i + 1) * ratio - 1 if causal else pl.num_programs(2) - 2
    first_near = i * ratio - 1 if causal else 0
    tk = k_ref.shape[0]
    dk = HEAD_DIM
    tq = q_ref.shape[0]
    rb = 256 if tq % 256 == 0 else tq
    units = [(mp, slice(r * rb, (r + 1) * rb)) for r in range(tq // rb) for mp in range(2)]

    @pl.when(j == 0)
    def _():
        q = q_ref[...] * (dk ** -0.5 * LOG2E)
        qs_ref[0] = q[:, :dk].astype(BF16)
        qs_ref[1] = q[:, dk:].astype(BF16)
        m_ref[...] = jnp.full(m_ref.shape, NEG_INF, F32)
        l_ref[...] = jnp.zeros(l_ref.shape, F32)
        acc_ref[...] = jnp.zeros(acc_ref.shape, F32)

    s_bufs = (sa_ref, sb_ref)

    def scores(slot, mp, rows):
        s_bufs[slot][mp, rows, :] = _dot_nt(qs_ref[mp, rows, :], k_ref[:, mp * dk:(mp + 1) * dk])

    def softmax(slot, mp, rows, bias, shift):
        s = s_bufs[slot][mp, rows, :]
        if bias is not None:
            s = s + bias[rows]
        m_prev = m_ref[mp, rows, :]
        m_next = jnp.maximum(m_prev, jnp.max(s, axis=1, keepdims=True) + shift)
        p = jnp.exp2(s - jnp.concatenate([m_next - shift] * (tk // LANES), axis=1))
        alpha = jnp.exp2(m_prev - m_next)
        l_ref[mp, rows, :] = alpha * l_ref[mp, rows, :] + jnp.sum(p, axis=1, keepdims=True)
        m_ref[mp, rows, :] = m_next
        return p.astype(BF16), alpha

    def values(mp, rows, p16, alpha):
        acc_ref[mp, rows, :] = (acc_ref[mp, rows, :] * jnp.concatenate([alpha] * (2 * dk // LANES), axis=1)
                                + _dot(p16, v_ref[...]))

    def step_body(cur, prev, bias, shift):
        pending = None
        for mp, rows in units:
            probs = softmax(prev, mp, rows, bias, shift) if prev is not None else None
            if cur is not None:
                scores(cur, mp, rows)
            if pending is not None:
                values(*pending)
            pending = (mp, rows, *probs) if probs is not None else None
        if pending is not None:
            values(*pending)

    @pl.when(j == 0)
    def _():
        step_body(0, None, None, 0.0)

    for cur in range(2):
        prev = 1 - cur
        par = (j % 2) == cur
        if causal:
            @pl.when(par & (j >= 1) & (j - 1 < first_near))
            def _():
                step_body(cur, prev, None, far_ref[pl.program_id(1)])

        @pl.when(par & (j >= 1) & (j - 1 >= first_near) & (j <= last))
        def _():
            step_body(cur, prev, bias_ref[...], 0.0)

        @pl.when(par & (j == last + 1))
        def _():
            step_body(None, prev, bias_ref[...], 0.0)

    @pl.when(j == last + 1)
    def _():
        lq = lq_ref[...]
        lam = (jnp.exp(jnp.sum(lq[0:1] * lq[1:2], axis=1, keepdims=True))
               - jnp.exp(jnp.sum(lq[2:3] * lq[3:4], axis=1, keepdims=True)) + lam0)
        reps = 2 * dk // LANES
        o0 = acc_ref[0] / jnp.concatenate([l_ref[0]] * reps, axis=1)
        o1 = acc_ref[1] / jnp.concatenate([l_ref[1]] * reps, axis=1)
        o_ref[...] = _rms(o0 - lam * o1, sub_ref[...]) * (1.0 - lam0)


def _diff_attention(q, k, v, bias, far, lam_qk, subln, lam0, *, n_heads, tq, tk, causal):
    bsz, tq_len, _ = q.shape
    tk_len = k.shape[1]
    nq, nk = tq_len // tq, tk_len // tk
    dv = 2 * HEAD_DIM
    assert tq % tk == 0 if causal else nq == 1
    ratio = tq // tk if causal else 1

    def last_tile(i):
        return (i + 1) * ratio - 1 if causal else nk - 1

    pairs = [(i, j) for i in range(nq) for j in range(last_tile(i) + 2)]
    i_tab = jnp.asarray([p[0] for p in pairs], jnp.int32)
    j_tab = jnp.asarray([p[1] for p in pairs], jnp.int32)

    def q_map(b, h, s, it, jt):
        return (b, it[s], h)

    def k_map(b, h, s, it, jt):
        return (b, jnp.minimum(jt[s], last_tile(it[s])), h)

    def v_map(b, h, s, it, jt):
        return (b, jnp.clip(jt[s] - 1, 0, last_tile(it[s])), h)

    def bias_map(b, h, s, it, jt):
        if causal:
            return (h, jnp.clip(jt[s] - it[s] * ratio, 0, ratio), 0, 0)
        return (h, jnp.clip(jt[s] - 1, 0, nk - 1), 0, 0)

    const2 = lambda b, h, s, it, jt: (0, 0)
    return pl.pallas_call(
        functools.partial(_diff_body, causal, ratio, lam0),
        grid_spec=pltpu.PrefetchScalarGridSpec(
            num_scalar_prefetch=2,
            grid=(bsz, n_heads, len(pairs)),
            in_specs=[
                pl.BlockSpec(memory_space=pltpu.SMEM),
                pl.BlockSpec((None, tq, dv), q_map),
                pl.BlockSpec((None, tk, dv), k_map),
                pl.BlockSpec((None, tk, dv), v_map),
                pl.BlockSpec((None, None, tq, tk), bias_map),
                pl.BlockSpec((4, HEAD_DIM), const2),
                pl.BlockSpec((1, dv), const2),
            ],
            out_specs=pl.BlockSpec((None, tq, dv), q_map),
            scratch_shapes=[pltpu.VMEM((2, tq, HEAD_DIM), BF16),
                            pltpu.VMEM((2, tq, tk), F32), pltpu.VMEM((2, tq, tk), F32),
                            pltpu.VMEM((2, tq, LANES), F32), pltpu.VMEM((2, tq, LANES), F32),
                            pltpu.VMEM((2, tq, dv), F32)],
        ),
        out_shape=jax.ShapeDtypeStruct((bsz, tq_len, n_heads * dv), F32),
        compiler_params=_params("parallel", "parallel", "arbitrary"),
        name="diff_attn",
    )(i_tab, j_tab, far, q, k, v, bias, lam_qk, subln.reshape(1, dv))


def _rel_bucket(rel):
    nb = N_BUCKETS // 2
    max_exact = nb // 2
    n = jnp.abs(rel)
    large = max_exact + (jnp.log(jnp.maximum(n, 1).astype(F32) / max_exact)
                         / math.log(MAX_DISTANCE / max_exact) * (nb - max_exact)).astype(jnp.int32)
    large = jnp.minimum(large, nb - 1)
    return jnp.where(rel > 0, nb, 0) + jnp.where(n < max_exact, n, large)


def _bias_window(rel_table, q0, nq, k0, nk):
    blk = min(LANES, nq)
    assert nq % blk == 0 and nk % blk == 0 and blk % CHUNK == 0
    na = nq // blk
    q_pos = q0 + jnp.arange(blk)[:, None]
    k_pos = k0 - (na - 1) * blk + jnp.arange(nk + (na - 1) * blk)[None, :]
    bucket = _rel_bucket(k_pos - q_pos)[None]
    bias = sum(jnp.where(bucket == b, rel_table[b][:, None, None], 0.0) for b in range(N_BUCKETS))
    wide = jnp.where(((k_pos // CHUNK) <= (q_pos // CHUNK))[None], bias, NEG_INF).astype(F32)
    return jnp.concatenate([wide[:, :, (na - 1 - a) * blk:(na - 1 - a) * blk + nk] for a in range(na)], axis=1)


def _mix_out_body(o_ref, qm_ref, mk_ref, mv_ref, w_ref, h_ref, g_ref, out_ref, mix_ref):
    n_o = o_ref.shape[1]
    mix_ref[:, 0:n_o] = o_ref[...].astype(BF16)
    for hd in range(MEM_HEADS):
        sl = slice(hd * HEAD_DIM, (hd + 1) * HEAD_DIM)
        s = _dot_nt(qm_ref[:, sl].astype(BF16), mk_ref[:, sl]) * (HEAD_DIM ** -0.5)
        e = jnp.exp(s - jnp.max(s, axis=1, keepdims=True))
        p = e / jnp.sum(e, axis=1, keepdims=True)
        mix_ref[:, n_o + hd * HEAD_DIM:n_o + (hd + 1) * HEAD_DIM] = _dot(p.astype(BF16), mv_ref[:, sl]).astype(BF16)
    y = _dot(mix_ref[...], w_ref[...])
    out_ref[...] = h_ref[...] + _rms(y, g_ref[...])


def _mix_out(o, qm_src, qm_block, mem_k, mem_v, w_out, h, g):
    bsz, tlen, n_o = o.shape
    d = h.shape[2]
    mlen = mem_k.shape[1]
    tm = _row_tile(tlen, ROW_TILE)
    return pl.pallas_call(
        _mix_out_body,
        grid=(bsz, tlen // tm),
        in_specs=[
            pl.BlockSpec((None, tm, n_o), lambda b, i: (b, i, 0)),
            pl.BlockSpec((None, tm, MEM_WIDTH), lambda b, i: (b, i, qm_block)),
            pl.BlockSpec((None, mlen, MEM_WIDTH), lambda b, i: (b, 0, 0)),
            pl.BlockSpec((None, mlen, MEM_WIDTH), lambda b, i: (b, 0, 0)),
            pl.BlockSpec((d, d), lambda b, i: (0, 0)),
            pl.BlockSpec((None, tm, d), lambda b, i: (b, i, 0)),
            pl.BlockSpec((1, d), lambda b, i: (0, 0)),
        ],
        out_specs=pl.BlockSpec((None, tm, d), lambda b, i: (b, i, 0)),
        out_shape=jax.ShapeDtypeStruct(h.shape, F32),
        scratch_shapes=[pltpu.VMEM((tm, d), BF16)],
        compiler_params=_params("parallel", "parallel"),
        name="mix_out",
    )(o, qm_src, mem_k, mem_v, w_out, h, g.reshape(1, d))


def _lambda_init(layer_idx):
    return 0.8 - 0.6 * math.exp(-0.3 * layer_idx)


def _prep_weights(norm_gains, ffn_gate_up, ffn_down, w_out, w_mem_kv, w_in_a, conv_w_a, w_kv, w_in_b):
    d = w_out.shape[1]
    n_a = w_in_a.shape[0]
    conv_dim = conv_w_a.shape[2]
    mix = d - MEM_WIDTH
    n_heads = mix // HEAD_DIM
    qkvz = conv_dim + mix
    w = {
        "gu": ffn_gate_up.astype(BF16), "down": ffn_down.astype(BF16), "out": w_out.astype(BF16),
        "mem_k": w_mem_kv[:, :, :MEM_WIDTH].astype(BF16), "mem_v": w_mem_kv[:, :, MEM_WIDTH:].astype(BF16),
        "in_a": jnp.concatenate([w_in_a[:, :, :qkvz], w_in_a[:, :, qkvz + 2 * n_heads:]], axis=2).astype(BF16),
        "in_a_ba": jnp.pad(w_in_a[:, :, qkvz:qkvz + 2 * n_heads],
                           ((0, 0), (0, 0), (0, LANES - 2 * n_heads))).astype(BF16),
        "kv_k": w_kv[:, :mix].astype(BF16), "kv_v": w_kv[:, mix:].astype(BF16),
        "in_b": w_in_b.astype(BF16),
    }
    return w, n_a, n_heads, mix


def _run_group(x, mem_k, mem_v, conv_states, delta_states, past_k, past_v, q_pos0, W, P, *, blocked):
    w, n_a, n_heads, mix = W
    bsz, tlen, d = x.shape
    rows = bsz * tlen
    depth = P["norm_gains"].shape[0]
    diff_heads = mix // (2 * HEAD_DIM)
    h = x.reshape(rows, d)
    new_conv, new_delta = [], []
    k_new = v_new = None
    for l in range(depth):
        ng = P["norm_gains"][l]
        if l == n_a:
            k_new, v_new, k16, v16 = _proj(h, P["kv_norm"], [w["kv_k"], w["kv_v"]], tm=512, tn=768,
                                           with_bf16=True, split_first=2, name="proj_kv")
        h = _ffn(h, ng[0], ng[1], w["gu"], w["down"], l, 0)
        if l < n_a:
            proj, ba = _proj(h, ng[2], [w["in_a"][l]], w["in_a_ba"][l], tm=1024, tn=1664, name="proj_a")
            n_proj = proj.shape[1]
            proj = proj.reshape(bsz, tlen, n_proj)
            conv_dim = 3 * mix
            new_conv.append(proj[:, tlen - (CONV_WIDTH - 1):, :conv_dim])
            conv_buf = jnp.pad(conv_states[l], ((0, 0), (SUBLANES - (CONV_WIDTH - 1), 0), (0, 0)))
            o, s_out = _gdn(proj, ba.reshape(bsz, tlen, LANES), conv_buf, P["conv_w_a"][l], P["a_log"][l],
                            P["dt_bias"][l], P["onorm_a"][l], delta_states[l], nc=GDN_CHUNKS if tlen % (GDN_CHUNKS * CHUNK) == 0 else 1, hg=GDN_HEADS)
            new_delta.append(s_out)
            qm_src, qm_block = proj, (conv_dim + mix) // MEM_WIDTH
        else:
            jb = l - n_a
            (proj,) = _proj(h, ng[2], [w["in_b"][jb]], tm=512, tn=2048, name="proj_b")
            proj = proj.reshape(bsz, tlen, d)
            lam0 = _lambda_init(l)
            k3 = k16.reshape(bsz, tlen, mix)
            v3 = v16.reshape(bsz, tlen, mix)
            if blocked:
                tq = _row_tile(tlen, ATTN_Q_TILE)
                tk = _row_tile(tq, ATTN_K_TILE)
                band = tq // tk + 1
                near = _bias_window(P["rel_bias"], tq, tq, tq - tk, band * tk)
                bias = jnp.stack([near[:, :, n * tk:(n + 1) * tk] for n in range(band)], axis=1) * LOG2E
                assert tk >= MAX_DISTANCE and tq % CHUNK == 0
                far = P["rel_bias"][_rel_bucket(jnp.int32(-2 * MAX_DISTANCE))] * LOG2E
                o = _diff_attention(proj, k3, v3, bias, far, P["lambda_qk"][jb], P["subln_b"][jb], lam0,
                                    n_heads=diff_heads, tq=tq, tk=tk, causal=True)
            else:
                past = past_k.shape[1]
                total = past + tlen
                padded = -(-total // LANES) * LANES
                grow = lambda c, n: jnp.pad(jnp.concatenate([c.reshape(bsz, past, mix).astype(BF16), n], axis=1),
                                            ((0, 0), (0, padded - total), (0, 0)))
                k_pos = jnp.arange(padded)
                bias = _bias_window(P["rel_bias"], q_pos0, tlen, 0, padded)
                bias = jnp.where(k_pos[None, None, :] < total, bias, NEG_INF)[:, None] * LOG2E
                o = _diff_attention(proj, grow(past_k, k3), grow(past_v, v3), bias, jnp.zeros((diff_heads,), F32),
                                    P["lambda_qk"][jb], P["subln_b"][jb], lam0, n_heads=diff_heads, tq=tlen,
                                    tk=padded, causal=False)
            qm_src, qm_block = proj, mix // MEM_WIDTH
        h = _mix_out(o, qm_src, qm_block, mem_k[l], mem_v[l], w["out"][l], h.reshape(bsz, tlen, d), ng[3])
        h = _ffn(h.reshape(rows, d), ng[4], ng[5], w["gu"], w["down"], l, 1)
    return h.reshape(bsz, tlen, d), jnp.stack(new_conv), jnp.stack(new_delta), k_new, v_new


def kernel(x_prompt, x_sample, mem_prompt, cache_k, cache_v, cache_mem_k, cache_mem_v, state_delta, state_conv,
           norm_gains, ffn_gate_up, ffn_down, w_out, mem_norm, w_mem_kv, w_in_a, conv_w_a, a_log, dt_bias,
           onorm_a, kv_norm, w_kv, w_in_b, lambda_qk, subln_b, rel_bias):
    W = _prep_weights(norm_gains, ffn_gate_up, ffn_down, w_out, w_mem_kv, w_in_a, conv_w_a, w_kv, w_in_b)
    w, n_a, n_heads, mix = W
    P = {"norm_gains": norm_gains, "kv_norm": kv_norm, "conv_w_a": conv_w_a, "a_log": a_log, "dt_bias": dt_bias,
         "onorm_a": onorm_a, "lambda_qk": lambda_qk, "subln_b": subln_b, "rel_bias": rel_bias}
    depth = norm_gains.shape[0]
    bp, tp, d = x_prompt.shape
    bs, ts, _ = x_sample.shape
    mlen = mem_prompt.shape[1]
    diff_heads = mix // (2 * HEAD_DIM)

    mem_rows = mem_prompt.reshape(bp * mlen, d)
    mem_kv = [_proj(mem_rows, mem_norm[l], [w["mem_k"][l], w["mem_v"][l]], tm=256, tn=512, name="proj_mem")
              for l in range(depth)]
    mem_k_p = jnp.stack([kv[0] for kv in mem_kv]).reshape(depth, bp, mlen, MEM_HEADS, HEAD_DIM)
    mem_v_p = jnp.stack([kv[1] for kv in mem_kv]).reshape(depth, bp, mlen, MEM_HEADS, HEAD_DIM)
    conv0 = jnp.zeros((n_a, bp, CONV_WIDTH - 1, 3 * mix), F32)
    delta0 = jnp.zeros((n_a, bp, n_heads, HEAD_DIM, HEAD_DIM), F32)
    flat = lambda m: m.reshape(m.shape[0], m.shape[1], m.shape[2], MEM_WIDTH).astype(BF16)
    y_p, conv_p, delta_p, k_p, v_p = _run_group(
        x_prompt, flat(mem_k_p), flat(mem_v_p), conv0, delta0, None, None, 0, W, P, blocked=True)

    y_s, conv_s, delta_s, k_s, v_s = _run_group(
        x_sample, flat(cache_mem_k), flat(cache_mem_v), state_conv, state_delta, cache_k, cache_v,
        cache_k.shape[1], W, P, blocked=False)

    shape_k = lambda a, b, t: a.reshape(b, t, diff_heads, 2, HEAD_DIM)
    shape_v = lambda a, b, t: a.reshape(b, t, diff_heads, 2 * HEAD_DIM)
    return (y_p, y_s, delta_p, conv_p, shape_k(k_p, bp, tp), shape_v(v_p, bp, tp), mem_k_p, mem_v_p,
            delta_s, conv_s, shape_k(k_s, bs, ts), shape_v(v_s, bs, ts))
```
